```python
import math
import jax, jax.numpy as jnp
from jax import lax
import numpy as np

D_MODEL = 1024
BATCH = 16
SEQ = 2048
DEPTH = 1

MIX_WIDTH = D_MODEL
SB_WIDTH = D_MODEL // 2
SB_HEADS = 8
SB_HEAD_DIM = SB_WIDTH // SB_HEADS
POOL_WIDTH = MIX_WIDTH - SB_WIDTH
POOL_WINDOWS = (2, 4, 8, 16)
POOL_GROUPS = len(POOL_WINDOWS)
POOL_GROUP_DIM = POOL_WIDTH // POOL_GROUPS
IN_WIDTH = 3 * SB_WIDTH + POOL_WIDTH
D_FF = -(-8 * D_MODEL // (3 * 256)) * 256
Q_BLOCK = 128
N_MOD = 6
EPS = 1e-6

kernel_name = "hybrid_stickbreak_pool_block"


def rmsnorm(x, g):
    xf = x.astype(jnp.float32)
    y = xf * lax.rsqrt(jnp.mean(xf * xf, axis=-1, keepdims=True) + EPS)
    return (y * g.astype(jnp.float32)).astype(x.dtype)


def stick_breaking_attention(q, k, v):
    S = q.shape[2]
    inv_sqrt = 1.0 / math.sqrt(q.shape[-1])
    outs = []
    for i in range(S // Q_BLOCK):
        L = (i + 1) * Q_BLOCK
        qb = q[:, :, i * Q_BLOCK:L].astype(jnp.float32)
        kb = k[:, :, :L].astype(jnp.float32)
        vb = v[:, :, :L].astype(jnp.float32)
        z = jnp.einsum('bhqd,bhkd->bhqk', qb, kb) * inv_sqrt
        t_idx = i * Q_BLOCK + jnp.arange(Q_BLOCK)[:, None]
        s_idx = jnp.arange(L)[None, :]
        mask = s_idx < t_idx
        log1m = jnp.where(mask, -jax.nn.softplus(z), 0.0)
        after = lax.cumsum(log1m, axis=3, reverse=True) - log1m
        logw = jax.nn.log_sigmoid(z) + after
        w = jnp.where(mask, jnp.exp(jnp.where(mask, logw, 0.0)), 0.0)
        outs.append(jnp.einsum('bhqk,bhkd->bhqd', w, vb))
    return jnp.concatenate(outs, axis=2).astype(v.dtype)


def pooling_mixer(u, w_pool, pool_scale):
    B, S, P = u.shape
    uf = u.astype(jnp.float32)
    cs = jnp.concatenate([jnp.zeros((B, 1, P), jnp.float32), jnp.cumsum(uf, axis=1)], axis=1)
    t = jnp.arange(S)
    parts = []
    for g, win in enumerate(POOL_WINDOWS):
        sl = slice(g * POOL_GROUP_DIM, (g + 1) * POOL_GROUP_DIM)
        lo = jnp.maximum(t + 1 - win, 0)
        cnt = (t + 1 - lo).astype(jnp.float32)
        csg = cs[..., sl]
        mean = (csg[:, 1:] - csg[:, lo]) / cnt[None, :, None]
        parts.append(mean - uf[..., sl])
    pooled = jnp.stack(parts, axis=2)
    y = jnp.einsum('bsgc,gcd->bsgd', pooled, w_pool.astype(jnp.float32)).reshape(B, S, P)
    return (y * pool_scale.astype(jnp.float32)).astype(u.dtype)


def setup_inputs(seed: int = 0) -> dict:
    key = jax.random.key(seed)
    ks = jax.random.split(key, 16)
    f32 = jnp.float32

    def nrm(k, shape, fan_in):
        return jax.random.normal(k, shape, f32) * fan_in ** -0.5

    def gain(k):
        return 1.0 + 0.05 * jax.random.normal(k, (DEPTH, D_MODEL), f32)

    return {
        "x": jax.random.normal(ks[0], (BATCH, SEQ, D_MODEL), f32),
        "c": jax.random.normal(ks[1], (BATCH, D_MODEL), f32),
        "w_cond": nrm(ks[2], (DEPTH, D_MODEL, N_MOD * D_MODEL), D_MODEL),
        "b_cond": 0.01 * jax.random.normal(ks[3], (DEPTH, N_MOD * D_MODEL), f32),
        "g_mix_pre": gain(ks[4]),
        "g_mix_post": gain(ks[5]),
        "w_in": nrm(ks[6], (DEPTH, D_MODEL, IN_WIDTH), D_MODEL),
        "w_pool": nrm(ks[7], (DEPTH, POOL_GROUPS, POOL_GROUP_DIM, POOL_GROUP_DIM), POOL_GROUP_DIM),
        "pool_scale": 1.0 + 0.1 * jax.random.normal(ks[8], (DEPTH, POOL_WIDTH), f32),
        "w_out": nrm(ks[9], (DEPTH, MIX_WIDTH, D_MODEL), MIX_WIDTH),
        "g_ffn_pre": gain(ks[10]),
        "g_ffn_post": gain(ks[11]),
        "w_gate": nrm(ks[12], (DEPTH, D_MODEL, D_FF), D_MODEL),
        "w_up": nrm(ks[13], (DEPTH, D_MODEL, D_FF), D_MODEL),
        "w_down": nrm(ks[14], (DEPTH, D_FF, D_MODEL), D_FF),
    }


def reference(x, c, w_cond, b_cond, g_mix_pre, g_mix_post, w_in, w_pool, pool_scale,
              w_out, g_ffn_pre, g_ffn_post, w_gate, w_up, w_down):
    B, S, D = x.shape
    for l in range(DEPTH):
        mod = jax.nn.silu(c) @ w_cond[l] + b_cond[l]
        shift_m, scale_m, gate_m, shift_f, scale_f, gate_f = [
            m[:, None, :] for m in jnp.split(mod, N_MOD, axis=-1)]

        h = rmsnorm(x, g_mix_pre[l]) * (1.0 + scale_m) + shift_m
        proj = h @ w_in[l]
        q, k, v, u = jnp.split(proj, [SB_WIDTH, 2 * SB_WIDTH, 3 * SB_WIDTH], axis=-1)
        to_heads = lambda a: a.reshape(B, S, SB_HEADS, SB_HEAD_DIM).transpose(0, 2, 1, 3)
        attn = stick_breaking_attention(to_heads(q), to_heads(k), to_heads(v))
        attn = attn.transpose(0, 2, 1, 3).reshape(B, S, SB_WIDTH)
        pool = pooling_mixer(u, w_pool[l], pool_scale[l])
        mix = jnp.concatenate([attn, pool], axis=-1) @ w_out[l]
        x = x + gate_m * rmsnorm(mix, g_mix_post[l])

        h = rmsnorm(x, g_ffn_pre[l]) * (1.0 + scale_f) + shift_f
        f = (jax.nn.silu(h @ w_gate[l]) * (h @ w_up[l])) @ w_down[l]
        x = x + gate_f * rmsnorm(f, g_ffn_post[l])
    return x
```

```python
import functools
import math

import jax
import jax.numpy as jnp
import numpy as np
from jax import lax
from jax.experimental import pallas as pl
from jax.experimental.pallas import tpu as pltpu

D_MODEL = 1024
SB_WIDTH = 512
SB_HEADS = 8
SB_HEAD_DIM = 64
POOL_WIDTH = 512
POOL_WINDOWS = (2, 4, 8, 16)
POOL_GROUP_DIM = 128
D_FF = 2816
N_MOD = 6
EPS = 1e-6

LANES = 128
Q_TILE = 128
K_TILE = 256
HEADS_PER_GROUP = LANES // SB_HEAD_DIM
N_GROUPS = SB_HEADS // HEADS_PER_GROUP
TOKEN_TILE = 512
POOL_HALO = 16
FF_CHUNK = 256
VMEM_LIMIT = 56 * 1024 * 1024

F32 = jnp.float32
BF16 = jnp.bfloat16


def _split_bf16(a):
    hi = a.astype(BF16)
    lo = (a - hi.astype(F32)).astype(BF16)
    return hi, lo


def _rms(x, g):
    return x * lax.rsqrt(jnp.mean(x * x, axis=-1, keepdims=True) + EPS) * g


def _cond_kernel(c_ref, w_ref, b_ref, o_ref):
    c = c_ref[...]
    a = c * jax.nn.sigmoid(c)
    a_hi, a_lo = _split_bf16(a)
    w_hi, w_lo = _split_bf16(w_ref[...])
    acc = jnp.dot(a_hi, w_hi, preferred_element_type=F32)
    acc += jnp.dot(a_lo, w_hi, preferred_element_type=F32)
    acc += jnp.dot(a_hi, w_lo, preferred_element_type=F32)
    o_ref[...] = acc + b_ref[...]


def _cond(c, w_cond, b_cond):
    B, D = c.shape
    N = w_cond.shape[1]
    tn = 1024
    return pl.pallas_call(
        _cond_kernel,
        grid=(N // tn,),
        in_specs=[
            pl.BlockSpec((B, D), lambda n: (0, 0)),
            pl.BlockSpec((D, tn), lambda n: (0, n)),
            pl.BlockSpec((1, tn), lambda n: (0, n)),
        ],
        out_specs=pl.BlockSpec((B, tn), lambda n: (0, n)),
        out_shape=jax.ShapeDtypeStruct((B, N), F32),
        compiler_params=pltpu.CompilerParams(
            dimension_semantics=("arbitrary",), vmem_limit_bytes=VMEM_LIMIT),
        name="cond",
    )(c, w_cond, b_cond.reshape(1, N))


def _proj_kernel(x_ref, mod_ref, g_ref, w_ref, q_ref, k_ref, vt_ref, u_ref):
    x = x_ref[0]
    shift = mod_ref[0, :, 0:D_MODEL]
    scale = mod_ref[0, :, D_MODEL:2 * D_MODEL]
    h = (_rms(x, g_ref[...]) * (1.0 + scale) + shift).astype(BF16)
    inv_sqrt = 1.0 / math.sqrt(SB_HEAD_DIM)
    q = jnp.dot(h, w_ref[:, 0:SB_WIDTH], preferred_element_type=F32)
    q_ref[0] = (q * inv_sqrt).astype(BF16)
    k = jnp.dot(h, w_ref[:, SB_WIDTH:2 * SB_WIDTH], preferred_element_type=F32)
    k_ref[0] = k.astype(BF16)
    v = jnp.dot(h, w_ref[:, 2 * SB_WIDTH:3 * SB_WIDTH], preferred_element_type=F32)
    for t in range(TOKEN_TILE // K_TILE):
        vt_ref[0, t] = v[t * K_TILE:(t + 1) * K_TILE, :].T.astype(BF16)
    u_ref[0] = jnp.dot(h, w_ref[:, 3 * SB_WIDTH:], preferred_element_type=F32)


def _proj(x, mod3, g_pre, w_in_bf):
    B, S, D = x.shape
    tm = TOKEN_TILE
    n_kt = S // K_TILE
    return pl.pallas_call(
        _proj_kernel,
        grid=(B, S // tm),
        in_specs=[
            pl.BlockSpec((1, tm, D), lambda b, t: (b, t, 0)),
            pl.BlockSpec((1, 1, N_MOD * D), lambda b, t: (b, 0, 0)),
            pl.BlockSpec((1, D), lambda b, t: (0, 0)),
            pl.BlockSpec(w_in_bf.shape, lambda b, t: (0, 0)),
        ],
        out_specs=[
            pl.BlockSpec((1, tm, SB_WIDTH), lambda b, t: (b, t, 0)),
            pl.BlockSpec((1, tm, SB_WIDTH), lambda b, t: (b, t, 0)),
            pl.BlockSpec((1, tm // K_TILE, SB_WIDTH, K_TILE), lambda b, t: (b, t, 0, 0)),
            pl.BlockSpec((1, tm, POOL_WIDTH), lambda b, t: (b, t, 0)),
        ],
        out_shape=[
            jax.ShapeDtypeStruct((B, S, SB_WIDTH), BF16),
            jax.ShapeDtypeStruct((B, S, SB_WIDTH), BF16),
            jax.ShapeDtypeStruct((B, n_kt, SB_WIDTH, K_TILE), BF16),
            jax.ShapeDtypeStruct((B, S, POOL_WIDTH), F32),
        ],
        compiler_params=pltpu.CompilerParams(
            dimension_semantics=("arbitrary", "arbitrary"), vmem_limit_bytes=VMEM_LIMIT),
        name="proj",
    )(x, mod3, g_pre, w_in_bf)


def _attn_kernel(q_ref, k_ref, vt_ref, a_ref, o_ref, acc_ref, carry_ref):
    i = pl.program_id(1)
    q = q_ref[0]
    lane = lax.broadcasted_iota(jnp.int32, (Q_TILE, LANES), 1)
    zero = jnp.zeros((), BF16)
    qbd = []
    for g in range(N_GROUPS):
        qg = q[:, g * LANES:(g + 1) * LANES]
        parts = [jnp.where((lane // SB_HEAD_DIM) == hh, qg, zero) for hh in range(HEADS_PER_GROUP)]
        qbd.append(jnp.concatenate(parts, axis=0))

    acc_ref[...] = jnp.zeros_like(acc_ref)
    carry_ref[...] = jnp.zeros_like(carry_ref)
    a_mat = a_ref[...]
    W = HEADS_PER_GROUP * Q_TILE

    def step(j, masked):
        row0 = pl.multiple_of(j * K_TILE, K_TILE)
        if masked:
            s_idx = row0 + lax.broadcasted_iota(jnp.int32, (K_TILE, W), 0)
            t_idx = i * Q_TILE + lax.broadcasted_iota(jnp.int32, (K_TILE, W), 1) % Q_TILE
            mask = s_idx < t_idx
        for g in range(N_GROUPS):
            kj = k_ref[0, pl.ds(row0, K_TILE), g * LANES:(g + 1) * LANES]
            z = lax.dot_general(kj, qbd[g], (((1,), (1,)), ((), ())),
                                preferred_element_type=F32)
            sp = jnp.maximum(z, 0.0) + jnp.log(1.0 + jnp.exp(-jnp.abs(z)))
            spm = jnp.where(mask, sp, 0.0) if masked else sp
            hi, lo = _split_bf16(spm)
            cs = (jnp.dot(a_mat, hi, preferred_element_type=F32)
                  + jnp.dot(a_mat, lo, preferred_element_type=F32))
            carry = carry_ref[g]
            w = jnp.exp(z - sp + cs + carry)
            if masked:
                w = jnp.where(mask, w, 0.0)
            vt = vt_ref[0, j, g * LANES:(g + 1) * LANES, :]
            acc_ref[g] += jnp.dot(vt, w.astype(BF16), preferred_element_type=F32)
            carry_ref[g] = carry + cs[0:1, :] - spm[0:1, :]

    j_diag = (i * Q_TILE) // K_TILE
    step(j_diag, True)

    def body(it, c):
        step(j_diag - 1 - it, False)
        return c

    lax.fori_loop(0, j_diag, body, 0)

    outs = []
    for g in range(N_GROUPS):
        acc = acc_ref[g]
        sel = jnp.concatenate(
            [acc[hh * SB_HEAD_DIM:(hh + 1) * SB_HEAD_DIM, hh * Q_TILE:(hh + 1) * Q_TILE]
             for hh in range(HEADS_PER_GROUP)], axis=0)
        outs.append(sel.T)
    o_ref[0] = jnp.concatenate(outs, axis=1).astype(o_ref.dtype)


def _cumsum_matrix():
    r = np.arange(K_TILE)
    return jnp.asarray(np.where(r[None, :] > r[:, None], -1.0, 0.0), dtype=BF16)


def _attention(q, k, vt):
    B, S, _ = q.shape
    n_kt = S // K_TILE
    W = HEADS_PER_GROUP * Q_TILE
    return pl.pallas_call(
        _attn_kernel,
        grid=(B, S // Q_TILE),
        in_specs=[
            pl.BlockSpec((1, Q_TILE, SB_WIDTH), lambda b, i: (b, i, 0)),
            pl.BlockSpec((1, S, SB_WIDTH), lambda b, i: (b, 0, 0)),
            pl.BlockSpec((1, n_kt, SB_WIDTH, K_TILE), lambda b, i: (b, 0, 0, 0)),
            pl.BlockSpec((K_TILE, K_TILE), lambda b, i: (0, 0)),
        ],
        out_specs=pl.BlockSpec((1, Q_TILE, SB_WIDTH), lambda b, i: (b, i, 0)),
        out_shape=jax.ShapeDtypeStruct((B, S, SB_WIDTH), BF16),
        scratch_shapes=[
            pltpu.VMEM((N_GROUPS, LANES, W), F32),
            pltpu.VMEM((N_GROUPS, 1, W), F32),
        ],
        compiler_params=pltpu.CompilerParams(
            dimension_semantics=("arbitrary", "arbitrary"), vmem_limit_bytes=VMEM_LIMIT),
        name="attn",
    )(q, k, vt, _cumsum_matrix())


def _mix_kernel(attn_ref, u_ref, uh_ref, x_ref, mod_ref, wp_ref, ps_ref, wo_ref, g_ref,
                o_ref, ext_ref):
    t = pl.program_id(1)
    tm = TOKEN_TILE
    u = u_ref[0]
    ext_ref[0:POOL_HALO, :] = jnp.where(t > 0, uh_ref[0], 0.0)
    ext_ref[POOL_HALO:, :] = u
    row = t * tm + lax.broadcasted_iota(jnp.int32, (tm, 1), 0)
    mix = jnp.dot(attn_ref[0], wo_ref[0:SB_WIDTH, :], preferred_element_type=F32)
    for g, win in enumerate(POOL_WINDOWS):
        cols = slice(g * POOL_GROUP_DIM, (g + 1) * POOL_GROUP_DIM)
        s = u[:, cols]
        for d in range(1, win):
            s = s + ext_ref[POOL_HALO - d:POOL_HALO - d + tm, cols]
        inv_cnt = 1.0 / jnp.minimum(row + 1, win).astype(F32)
        pooled = s * inv_cnt - u[:, cols]
        y = jnp.dot(pooled.astype(BF16), wp_ref[g], preferred_element_type=F32) * ps_ref[:, cols]
        mix += jnp.dot(y.astype(BF16), wo_ref[SB_WIDTH + g * POOL_GROUP_DIM:
                                              SB_WIDTH + (g + 1) * POOL_GROUP_DIM, :],
                       preferred_element_type=F32)
    gate = mod_ref[0, :, 2 * D_MODEL:3 * D_MODEL]
    o_ref[0] = x_ref[0] + gate * _rms(mix, g_ref[...])


def _mix(attn, u, x, mod3, w_pool_bf, pool_scale, w_out_bf, g_post):
    B, S, D = x.shape
    tm = TOKEN_TILE
    halo_blocks = tm // POOL_HALO
    return pl.pallas_call(
        _mix_kernel,
        grid=(B, S // tm),
        in_specs=[
            pl.BlockSpec((1, tm, SB_WIDTH), lambda b, t: (b, t, 0)),
            pl.BlockSpec((1, tm, POOL_WIDTH), lambda b, t: (b, t, 0)),
            pl.BlockSpec((1, POOL_HALO, POOL_WIDTH),
                         lambda b, t: (b, jnp.maximum(t * halo_blocks - 1, 0), 0)),
            pl.BlockSpec((1, tm, D), lambda b, t: (b, t, 0)),
            pl.BlockSpec((1, 1, N_MOD * D), lambda b, t: (b, 0, 0)),
            pl.BlockSpec(w_pool_bf.shape, lambda b, t: (0, 0, 0)),
            pl.BlockSpec((1, POOL_WIDTH), lambda b, t: (0, 0)),
            pl.BlockSpec(w_out_bf.shape, lambda b, t: (0, 0)),
            pl.BlockSpec((1, D), lambda b, t: (0, 0)),
        ],
        out_specs=pl.BlockSpec((1, tm, D), lambda b, t: (b, t, 0)),
        out_shape=jax.ShapeDtypeStruct((B, S, D), F32),
        scratch_shapes=[pltpu.VMEM((tm + POOL_HALO, POOL_WIDTH), F32)],
        compiler_params=pltpu.CompilerParams(
            dimension_semantics=("arbitrary", "arbitrary"), vmem_limit_bytes=VMEM_LIMIT),
        name="mix",
    )(attn, u, u, x, mod3, w_pool_bf, pool_scale, w_out_bf, g_post)


def _ffn_kernel(x_ref, mod_ref, gpre_ref, gpost_ref, wg_ref, wu_ref, wd_ref, o_ref):
    x = x_ref[0]
    shift = mod_ref[0, :, 3 * D_MODEL:4 * D_MODEL]
    scale = mod_ref[0, :, 4 * D_MODEL:5 * D_MODEL]
    gate = mod_ref[0, :, 5 * D_MODEL:6 * D_MODEL]
    h = (_rms(x, gpre_ref[...]) * (1.0 + scale) + shift).astype(BF16)
    f = jnp.zeros((TOKEN_TILE, D_MODEL), F32)
    for c in range(0, D_FF, FF_CHUNK):
        a = jnp.dot(h, wg_ref[:, c:c + FF_CHUNK], preferred_element_type=F32)
        b = jnp.dot(h, wu_ref[:, c:c + FF_CHUNK], preferred_element_type=F32)
        act = (a * jax.nn.sigmoid(a) * b).astype(BF16)
        f += jnp.dot(act, wd_ref[c:c + FF_CHUNK, :], preferred_element_type=F32)
    o_ref[0] = x + gate * _rms(f, gpost_ref[...])


def _ffn(x1, mod3, g_pre, g_post, wg_bf, wu_bf, wd_bf):
    B, S, D = x1.shape
    tm = TOKEN_TILE
    resident = functools.partial(pl.BlockSpec, pipeline_mode=pl.Buffered(1))
    return pl.pallas_call(
        _ffn_kernel,
        grid=(B, S // tm),
        in_specs=[
            pl.BlockSpec((1, tm, D), lambda b, t: (b, t, 0)),
            pl.BlockSpec((1, 1, N_MOD * D), lambda b, t: (b, 0, 0)),
            pl.BlockSpec((1, D), lambda b, t: (0, 0)),
            pl.BlockSpec((1, D), lambda b, t: (0, 0)),
            resident(wg_bf.shape, lambda b, t: (0, 0)),
            resident(wu_bf.shape, lambda b, t: (0, 0)),
            resident(wd_bf.shape, lambda b, t: (0, 0)),
        ],
        out_specs=pl.BlockSpec((1, tm, D), lambda b, t: (b, t, 0)),
        out_shape=jax.ShapeDtypeStruct((B, S, D), F32),
        compiler_params=pltpu.CompilerParams(
            dimension_semantics=("arbitrary", "arbitrary"), vmem_limit_bytes=VMEM_LIMIT),
        name="ffn",
    )(x1, mod3, g_pre, g_post, wg_bf, wu_bf, wd_bf)


def kernel(x, c, w_cond, b_cond, g_mix_pre, g_mix_post, w_in, w_pool, pool_scale, w_out,
           g_ffn_pre, g_ffn_post, w_gate, w_up, w_down):
    B, S, D = x.shape
    depth = w_cond.shape[0]
    for l in range(depth):
        mod3 = _cond(c, w_cond[l], b_cond[l]).reshape(B, 1, N_MOD * D)
        q, k, vt, u = _proj(x, mod3, g_mix_pre[l].reshape(1, D), w_in[l].astype(BF16))
        attn = _attention(q, k, vt)
        x = _mix(attn, u, x, mod3, w_pool[l].astype(BF16), pool_scale[l].reshape(1, POOL_WIDTH),
                 w_out[l].astype(BF16), g_mix_post[l].reshape(1, D))
        x = _ffn(x, mod3, g_ffn_pre[l].reshape(1, D), g_ffn_post[l].reshape(1, D),
                 w_gate[l].astype(BF16), w_up[l].astype(BF16), w_down[l].astype(BF16))
    return x
```

```python
import functools
import math

import jax
import jax.numpy as jnp
import numpy as np
from jax import lax
from jax.experimental import pallas as pl
from jax.experimental.pallas import tpu as pltpu

D_MODEL = 1024
SB_WIDTH = 512
SB_HEADS = 8
SB_HEAD_DIM = 64
POOL_WIDTH = 512
POOL_WINDOWS = (2, 4, 8, 16)
POOL_GROUP_DIM = 128
D_FF = 2816
N_MOD = 6
EPS = 1e-6

LANES = 128
Q_TILE = 256
K_TILE = 256
PV_LAG = 2
DEAD_LOG = -104.0
HEADS_PER_GROUP = LANES // SB_HEAD_DIM
N_GROUPS = SB_HEADS // HEADS_PER_GROUP
TOKEN_TILE = 512
POOL_HALO = 16
FF_CHUNK = 256
VMEM_LIMIT = 56 * 1024 * 1024

F32 = jnp.float32
BF16 = jnp.bfloat16


def _split_bf16(a):
    hi = a.astype(BF16)
    lo = (a - hi.astype(F32)).astype(BF16)
    return hi, lo


def _rms(x, g):
    return x * lax.rsqrt(jnp.mean(x * x, axis=-1, keepdims=True) + EPS) * g


def _cond_kernel(c_ref, w_ref, b_ref, o_ref):
    c = c_ref[...]
    a = c * jax.nn.sigmoid(c)
    a_hi, a_lo = _split_bf16(a)
    w_hi, w_lo = _split_bf16(w_ref[...])
    acc = jnp.dot(a_hi, w_hi, preferred_element_type=F32)
    acc += jnp.dot(a_lo, w_hi, preferred_element_type=F32)
    acc += jnp.dot(a_hi, w_lo, preferred_element_type=F32)
    o_ref[...] = acc + b_ref[...]


def _cond(c, w_cond, b_cond):
    B, D = c.shape
    N = w_cond.shape[1]
    tn = 1024
    return pl.pallas_call(
        _cond_kernel,
        grid=(N // tn,),
        in_specs=[
            pl.BlockSpec((B, D), lambda n: (0, 0)),
            pl.BlockSpec((D, tn), lambda n: (0, n)),
            pl.BlockSpec((1, tn), lambda n: (0, n)),
        ],
        out_specs=pl.BlockSpec((B, tn), lambda n: (0, n)),
        out_shape=jax.ShapeDtypeStruct((B, N), F32),
        compiler_params=pltpu.CompilerParams(
            dimension_semantics=("arbitrary",), vmem_limit_bytes=VMEM_LIMIT),
        name="cond",
    )(c, w_cond, b_cond.reshape(1, N))


def _proj_kernel(x_ref, mod_ref, g_ref, w_ref, q_ref, k_ref, vt_ref, u_ref):
    x = x_ref[0]
    shift = mod_ref[0, :, 0:D_MODEL]
    scale = mod_ref[0, :, D_MODEL:2 * D_MODEL]
    h = (_rms(x, g_ref[...]) * (1.0 + scale) + shift).astype(BF16)
    inv_sqrt = 1.0 / math.sqrt(SB_HEAD_DIM)
    q = jnp.dot(h, w_ref[:, 0:SB_WIDTH], preferred_element_type=F32)
    q_ref[0] = (q * inv_sqrt).astype(BF16)
    k = jnp.dot(h, w_ref[:, SB_WIDTH:2 * SB_WIDTH], preferred_element_type=F32)
    k_ref[0] = k.astype(BF16)
    v = jnp.dot(h, w_ref[:, 2 * SB_WIDTH:3 * SB_WIDTH], preferred_element_type=F32)
    for t in range(TOKEN_TILE // K_TILE):
        vt_ref[0, t] = v[t * K_TILE:(t + 1) * K_TILE, :].T.astype(BF16)
    u_ref[0] = jnp.dot(h, w_ref[:, 3 * SB_WIDTH:], preferred_element_type=F32)


def _proj(x, mod3, g_pre, w_in_bf):
    B, S, D = x.shape
    tm = TOKEN_TILE
    n_kt = S // K_TILE
    return pl.pallas_call(
        _proj_kernel,
        grid=(B, S // tm),
        in_specs=[
            pl.BlockSpec((1, tm, D), lambda b, t: (b, t, 0)),
            pl.BlockSpec((1, 1, N_MOD * D), lambda b, t: (b, 0, 0)),
            pl.BlockSpec((1, D), lambda b, t: (0, 0)),
            pl.BlockSpec(w_in_bf.shape, lambda b, t: (0, 0)),
        ],
        out_specs=[
            pl.BlockSpec((1, tm, SB_WIDTH), lambda b, t: (b, t, 0)),
            pl.BlockSpec((1, tm, SB_WIDTH), lambda b, t: (b, t, 0)),
            pl.BlockSpec((1, tm // K_TILE, SB_WIDTH, K_TILE), lambda b, t: (b, t, 0, 0)),
            pl.BlockSpec((1, tm, POOL_WIDTH), lambda b, t: (b, t, 0)),
        ],
        out_shape=[
            jax.ShapeDtypeStruct((B, S, SB_WIDTH), BF16),
            jax.ShapeDtypeStruct((B, S, SB_WIDTH), BF16),
            jax.ShapeDtypeStruct((B, n_kt, SB_WIDTH, K_TILE), BF16),
            jax.ShapeDtypeStruct((B, S, POOL_WIDTH), F32),
        ],
        compiler_params=pltpu.CompilerParams(
            dimension_semantics=("arbitrary", "arbitrary"), vmem_limit_bytes=VMEM_LIMIT),
        name="proj",
    )(x, mod3, g_pre, w_in_bf)


def _attn_kernel(q_ref, k_ref, vt_ref, a_ref, o_ref, acc_ref, carry_ref):
    j_diag = pl.program_id(1)
    q = q_ref[0]
    lane = lax.broadcasted_iota(jnp.int32, (Q_TILE, LANES), 1)
    zero = jnp.zeros((), BF16)
    qh = []
    for h in range(SB_HEADS):
        g, hh = divmod(h, HEADS_PER_GROUP)
        qg = q[:, g * LANES:(g + 1) * LANES]
        qh.append(jnp.where((lane // SB_HEAD_DIM) == hh, qg, zero))

    acc_ref[...] = jnp.zeros_like(acc_ref)
    carry_ref[...] = jnp.zeros_like(carry_ref)
    a_mat = a_ref[...]
    diag_mask = (lax.broadcasted_iota(jnp.int32, (K_TILE, Q_TILE), 0)
                 < lax.broadcasted_iota(jnp.int32, (K_TILE, Q_TILE), 1))

    def step(j, masked):
        row0 = pl.multiple_of(j * K_TILE, K_TILE)
        kg = [k_ref[0, pl.ds(row0, K_TILE), g * LANES:(g + 1) * LANES] for g in range(N_GROUPS)]
        zs = [lax.dot_general(kg[h // HEADS_PER_GROUP], qh[h], (((1,), (1,)), ((), ())),
                              preferred_element_type=F32) for h in range(SB_HEADS)]

        def scores(h):
            z = zs[h]
            sp = jnp.maximum(z, 0.0) + jnp.log(1.0 + jnp.exp(-jnp.abs(z)))
            spm = jnp.where(diag_mask, sp, 0.0) if masked else sp
            hi, lo = _split_bf16(spm)
            cs = (jnp.dot(a_mat, hi, preferred_element_type=F32)
                  + jnp.dot(a_mat, lo, preferred_element_type=F32))
            return z - sp, cs, spm[0:1, :]

        def weights(h, logsig, cs, sp0):
            carry = carry_ref[h]
            w = jnp.exp(logsig + cs + carry)
            if masked:
                w = jnp.where(diag_mask, w, 0.0)
            vt = vt_ref[0, j, h * SB_HEAD_DIM:(h + 1) * SB_HEAD_DIM, :]
            acc_ref[h] += jnp.dot(vt, w.astype(BF16), preferred_element_type=F32)
            carry_ref[h] = carry + cs[0:1, :] - sp0

        pending = {}
        for s in range(SB_HEADS + PV_LAG):
            if s < SB_HEADS:
                pending[s] = scores(s)
            if s >= PV_LAG:
                weights(s - PV_LAG, *pending.pop(s - PV_LAG))

    def alive():
        return (jnp.max(carry_ref[...]) > DEAD_LOG).astype(jnp.int32)

    step(j_diag, True)

    def cond(state):
        n, live = state
        return jnp.logical_and(n < j_diag, live > 0)

    def body(state):
        n, _ = state
        step(j_diag - 1 - n, False)
        return n + 1, alive()

    lax.while_loop(cond, body, (jnp.int32(0), alive()))

    outs = []
    for g in range(N_GROUPS):
        sel = jnp.concatenate([acc_ref[g * HEADS_PER_GROUP + hh] for hh in range(HEADS_PER_GROUP)],
                              axis=0)
        outs.append(sel.T)
    o_ref[0] = jnp.concatenate(outs, axis=1).astype(o_ref.dtype)


def _cumsum_matrix():
    r = np.arange(K_TILE)
    return jnp.asarray(np.where(r[None, :] > r[:, None], -1.0, 0.0), dtype=BF16)


def _attention(q, k, vt):
    B, S, _ = q.shape
    n_kt = S // K_TILE
    return pl.pallas_call(
        _attn_kernel,
        grid=(B, S // Q_TILE),
        in_specs=[
            pl.BlockSpec((1, Q_TILE, SB_WIDTH), lambda b, i: (b, i, 0)),
            pl.BlockSpec((1, S, SB_WIDTH), lambda b, i: (b, 0, 0)),
            pl.BlockSpec((1, n_kt, SB_WIDTH, K_TILE), lambda b, i: (b, 0, 0, 0)),
            pl.BlockSpec((K_TILE, K_TILE), lambda b, i: (0, 0)),
        ],
        out_specs=pl.BlockSpec((1, Q_TILE, SB_WIDTH), lambda b, i: (b, i, 0)),
        out_shape=jax.ShapeDtypeStruct((B, S, SB_WIDTH), BF16),
        scratch_shapes=[
            pltpu.VMEM((SB_HEADS, SB_HEAD_DIM, Q_TILE), F32),
            pltpu.VMEM((SB_HEADS, 1, Q_TILE), F32),
        ],
        compiler_params=pltpu.CompilerParams(
            dimension_semantics=("arbitrary", "arbitrary"), vmem_limit_bytes=VMEM_LIMIT),
        name="attn",
    )(q, k, vt, _cumsum_matrix())


def _mix_kernel(attn_ref, u_ref, uh_ref, x_ref, mod_ref, wp_ref, ps_ref, wo_ref, g_ref,
                o_ref, ext_ref):
    t = pl.program_id(1)
    tm = TOKEN_TILE
    u = u_ref[0]
    ext_ref[0:POOL_HALO, :] = jnp.where(t > 0, uh_ref[0], 0.0)
    ext_ref[POOL_HALO:, :] = u
    row = t * tm + lax.broadcasted_iota(jnp.int32, (tm, 1), 0)
    mix = jnp.dot(attn_ref[0], wo_ref[0:SB_WIDTH, :], preferred_element_type=F32)
    for g, win in enumerate(POOL_WINDOWS):
        cols = slice(g * POOL_GROUP_DIM, (g + 1) * POOL_GROUP_DIM)
        s = u[:, cols]
        for d in range(1, win):
            s = s + ext_ref[POOL_HALO - d:POOL_HALO - d + tm, cols]
        inv_cnt = 1.0 / jnp.minimum(row + 1, win).astype(F32)
        pooled = s * inv_cnt - u[:, cols]
        y = jnp.dot(pooled.astype(BF16), wp_ref[g], preferred_element_type=F32) * ps_ref[:, cols]
        mix += jnp.dot(y.astype(BF16), wo_ref[SB_WIDTH + g * POOL_GROUP_DIM:
                                              SB_WIDTH + (g + 1) * POOL_GROUP_DIM, :],
                       preferred_element_type=F32)
    gate = mod_ref[0, :, 2 * D_MODEL:3 * D_MODEL]
    o_ref[0] = x_ref[0] + gate * _rms(mix, g_ref[...])


def _mix(attn, u, x, mod3, w_pool_bf, pool_scale, w_out_bf, g_post):
    B, S, D = x.shape
    tm = TOKEN_TILE
    halo_blocks = tm // POOL_HALO
    return pl.pallas_call(
        _mix_kernel,
        grid=(B, S // tm),
        in_specs=[
            pl.BlockSpec((1, tm, SB_WIDTH), lambda b, t: (b, t, 0)),
            pl.BlockSpec((1, tm, POOL_WIDTH), lambda b, t: (b, t, 0)),
            pl.BlockSpec((1, POOL_HALO, POOL_WIDTH),
                         lambda b, t: (b, jnp.maximum(t * halo_blocks - 1, 0), 0)),
            pl.BlockSpec((1, tm, D), lambda b, t: (b, t, 0)),
            pl.BlockSpec((1, 1, N_MOD * D), lambda b, t: (b, 0, 0)),
            pl.BlockSpec(w_pool_bf.shape, lambda b, t: (0, 0, 0)),
            pl.BlockSpec((1, POOL_WIDTH), lambda b, t: (0, 0)),
            pl.BlockSpec(w_out_bf.shape, lambda b, t: (0, 0)),
            pl.BlockSpec((1, D), lambda b, t: (0, 0)),
        ],
        out_specs=pl.BlockSpec((1, tm, D), lambda b, t: (b, t, 0)),
        out_shape=jax.ShapeDtypeStruct((B, S, D), F32),
        scratch_shapes=[pltpu.VMEM((tm + POOL_HALO, POOL_WIDTH), F32)],
        compiler_params=pltpu.CompilerParams(
            dimension_semantics=("arbitrary", "arbitrary"), vmem_limit_bytes=VMEM_LIMIT),
        name="mix",
    )(attn, u, u, x, mod3, w_pool_bf, pool_scale, w_out_bf, g_post)


def _ffn_kernel(x_ref, mod_ref, gpre_ref, gpost_ref, wg_ref, wu_ref, wd_ref, o_ref):
    x = x_ref[0]
    shift = mod_ref[0, :, 3 * D_MODEL:4 * D_MODEL]
    scale = mod_ref[0, :, 4 * D_MODEL:5 * D_MODEL]
    gate = mod_ref[0, :, 5 * D_MODEL:6 * D_MODEL]
    h = (_rms(x, gpre_ref[...]) * (1.0 + scale) + shift).astype(BF16)
    f = jnp.zeros((TOKEN_TILE, D_MODEL), F32)
    for c in range(0, D_FF, FF_CHUNK):
        a = jnp.dot(h, wg_ref[:, c:c + FF_CHUNK], preferred_element_type=F32)
        b = jnp.dot(h, wu_ref[:, c:c + FF_CHUNK], preferred_element_type=F32)
        act = (a * jax.nn.sigmoid(a) * b).astype(BF16)
        f += jnp.dot(act, wd_ref[c:c + FF_CHUNK, :], preferred_element_type=F32)
    o_ref[0] = x + gate * _rms(f, gpost_ref[...])


def _ffn(x1, mod3, g_pre, g_post, wg_bf, wu_bf, wd_bf):
    B, S, D = x1.shape
    tm = TOKEN_TILE
    resident = functools.partial(pl.BlockSpec, pipeline_mode=pl.Buffered(1))
    return pl.pallas_call(
        _ffn_kernel,
        grid=(B, S // tm),
        in_specs=[
            pl.BlockSpec((1, tm, D), lambda b, t: (b, t, 0)),
            pl.BlockSpec((1, 1, N_MOD * D), lambda b, t: (b, 0, 0)),
            pl.BlockSpec((1, D), lambda b, t: (0, 0)),
            pl.BlockSpec((1, D), lambda b, t: (0, 0)),
            resident(wg_bf.shape, lambda b, t: (0, 0)),
            resident(wu_bf.shape, lambda b, t: (0, 0)),
            resident(wd_bf.shape, lambda b, t: (0, 0)),
        ],
        out_specs=pl.BlockSpec((1, tm, D), lambda b, t: (b, t, 0)),
        out_shape=jax.ShapeDtypeStruct((B, S, D), F32),
        compiler_params=pltpu.CompilerParams(
            dimension_semantics=("arbitrary", "arbitrary"), vmem_limit_bytes=VMEM_LIMIT),
        name="ffn",
    )(x1, mod3, g_pre, g_post, wg_bf, wu_bf, wd_bf)


def kernel(x, c, w_cond, b_cond, g_mix_pre, g_mix_post, w_in, w_pool, pool_scale, w_out,
           g_ffn_pre, g_ffn_post, w_gate, w_up, w_down):
    B, S, D = x.shape
    depth = w_cond.shape[0]
    for l in range(depth):
        mod3 = _cond(c, w_cond[l], b_cond[l]).reshape(B, 1, N_MOD * D)
        q, k, vt, u = _proj(x, mod3, g_mix_pre[l].reshape(1, D), w_in[l].astype(BF16))
        attn = _attention(q, k, vt)
        x = _mix(attn, u, x, mod3, w_pool[l].astype(BF16), pool_scale[l].reshape(1, POOL_WIDTH),
                 w_out[l].astype(BF16), g_mix_post[l].reshape(1, D))
        x = _ffn(x, mod3, g_ffn_pre[l].reshape(1, D), g_ffn_post[l].reshape(1, D),
                 w_gate[l].astype(BF16), w_up[l].astype(BF16), w_down[l].astype(BF16))
    return x
```

```python
import functools
import math

import jax
import jax.numpy as jnp
import numpy as np
from jax import lax
from jax.experimental import pallas as pl
from jax.experimental.pallas import tpu as pltpu

D_MODEL = 1024
SB_WIDTH = 512
SB_HEADS = 8
SB_HEAD_DIM = 64
POOL_WIDTH = 512
POOL_WINDOWS = (2, 4, 8, 16)
POOL_GROUP_DIM = 128
D_FF = 2816
N_MOD = 6
EPS = 1e-6

LANES = 128
Q_TILE = 256
K_TILE = 256
LOG2E = math.log2(math.e)
SP_CLAMP = 100.0
DEAD_LOG2 = -151.0
HEADS_PER_GROUP = LANES // SB_HEAD_DIM
N_GROUPS = SB_HEADS // HEADS_PER_GROUP
TOKEN_TILE = 512
POOL_HALO = 16
FF_CHUNK = 256
VMEM_LIMIT = 56 * 1024 * 1024

F32 = jnp.float32
BF16 = jnp.bfloat16


def _split_bf16(a):
    hi = a.astype(BF16)
    lo = (a - hi.astype(F32)).astype(BF16)
    return hi, lo


def _rms(x, g):
    return x * lax.rsqrt(jnp.mean(x * x, axis=-1, keepdims=True) + EPS) * g


def _cond_kernel(c_ref, w_ref, b_ref, o_ref):
    c = c_ref[...]
    a = c * jax.nn.sigmoid(c)
    a_hi, a_lo = _split_bf16(a)
    w_hi, w_lo = _split_bf16(w_ref[...])
    acc = jnp.dot(a_hi, w_hi, preferred_element_type=F32)
    acc += jnp.dot(a_lo, w_hi, preferred_element_type=F32)
    acc += jnp.dot(a_hi, w_lo, preferred_element_type=F32)
    o_ref[...] = acc + b_ref[...]


def _cond(c, w_cond, b_cond):
    B, D = c.shape
    N = w_cond.shape[1]
    tn = 1024
    return pl.pallas_call(
        _cond_kernel,
        grid=(N // tn,),
        in_specs=[
            pl.BlockSpec((B, D), lambda n: (0, 0)),
            pl.BlockSpec((D, tn), lambda n: (0, n)),
            pl.BlockSpec((1, tn), lambda n: (0, n)),
        ],
        out_specs=pl.BlockSpec((B, tn), lambda n: (0, n)),
        out_shape=jax.ShapeDtypeStruct((B, N), F32),
        compiler_params=pltpu.CompilerParams(
            dimension_semantics=("arbitrary",), vmem_limit_bytes=VMEM_LIMIT),
        name="cond",
    )(c, w_cond, b_cond.reshape(1, N))


def _proj_kernel(x_ref, mod_ref, g_ref, w_ref, q_ref, k_ref, vt_ref, u_ref):
    x = x_ref[0]
    shift = mod_ref[0, :, 0:D_MODEL]
    scale = mod_ref[0, :, D_MODEL:2 * D_MODEL]
    h = (_rms(x, g_ref[...]) * (1.0 + scale) + shift).astype(BF16)
    q = jnp.dot(h, w_ref[:, 0:SB_WIDTH], preferred_element_type=F32)
    q_ref[0] = (q * (LOG2E / math.sqrt(SB_HEAD_DIM))).astype(BF16)
    k = jnp.dot(h, w_ref[:, SB_WIDTH:2 * SB_WIDTH], preferred_element_type=F32)
    k_ref[0] = k.astype(BF16)
    v = jnp.dot(h, w_ref[:, 2 * SB_WIDTH:3 * SB_WIDTH], preferred_element_type=F32)
    for t in range(TOKEN_TILE // K_TILE):
        vt_ref[0, t] = v[t * K_TILE:(t + 1) * K_TILE, :].T.astype(BF16)
    u_ref[0] = jnp.dot(h, w_ref[:, 3 * SB_WIDTH:], preferred_element_type=F32)


def _proj(x, mod3, g_pre, w_in_bf):
    B, S, D = x.shape
    tm = TOKEN_TILE
    n_kt = S // K_TILE
    return pl.pallas_call(
        _proj_kernel,
        grid=(B, S // tm),
        in_specs=[
            pl.BlockSpec((1, tm, D), lambda b, t: (b, t, 0)),
            pl.BlockSpec((1, 1, N_MOD * D), lambda b, t: (b, 0, 0)),
            pl.BlockSpec((1, D), lambda b, t: (0, 0)),
            pl.BlockSpec(w_in_bf.shape, lambda b, t: (0, 0)),
        ],
        out_specs=[
            pl.BlockSpec((1, tm, SB_WIDTH), lambda b, t: (b, t, 0)),
            pl.BlockSpec((1, tm, SB_WIDTH), lambda b, t: (b, t, 0)),
            pl.BlockSpec((1, tm // K_TILE, SB_WIDTH, K_TILE), lambda b, t: (b, t, 0, 0)),
            pl.BlockSpec((1, tm, POOL_WIDTH), lambda b, t: (b, t, 0)),
        ],
        out_shape=[
            jax.ShapeDtypeStruct((B, S, SB_WIDTH), BF16),
            jax.ShapeDtypeStruct((B, S, SB_WIDTH), BF16),
            jax.ShapeDtypeStruct((B, n_kt, SB_WIDTH, K_TILE), BF16),
            jax.ShapeDtypeStruct((B, S, POOL_WIDTH), F32),
        ],
        compiler_params=pltpu.CompilerParams(
            dimension_semantics=("arbitrary", "arbitrary"), vmem_limit_bytes=VMEM_LIMIT),
        name="proj",
    )(x, mod3, g_pre, w_in_bf)


def _attn_kernel(q_ref, k_ref, vt_ref, a_ref, o_ref, acc_ref, carry_ref, z_ref, x_ref, sc_ref):
    j_diag = pl.program_id(1)
    q = q_ref[0]
    lane = lax.broadcasted_iota(jnp.int32, (Q_TILE, LANES), 1)
    zero = jnp.zeros((), BF16)
    qh = []
    for h in range(SB_HEADS):
        g, hh = divmod(h, HEADS_PER_GROUP)
        qg = q[:, g * LANES:(g + 1) * LANES]
        qh.append(jnp.where((lane // SB_HEAD_DIM) == hh, qg, zero))

    a_mat = a_ref[...]
    diag_mask = (lax.broadcasted_iota(jnp.int32, (K_TILE, Q_TILE), 0)
                 < lax.broadcasted_iota(jnp.int32, (K_TILE, Q_TILE), 1))

    def logits(j, h):
        row0 = pl.multiple_of(j * K_TILE, K_TILE)
        g = h // HEADS_PER_GROUP
        kj = k_ref[0, pl.ds(row0, K_TILE), g * LANES:(g + 1) * LANES]
        return lax.dot_general(kj, qh[h], (((1,), (1,)), ((), ())), preferred_element_type=F32)

    def values(j, h):
        return vt_ref[0, j, h * SB_HEAD_DIM:(h + 1) * SB_HEAD_DIM, :]

    def step(j, masked):
        j_next = jnp.maximum(j - 1, 0)
        for s in range(SB_HEADS):
            z = z_ref[s]
            z_ref[s] = logits(j_next, s)
            sp = jnp.maximum(z, jnp.log(1.0 + jnp.exp2(jnp.minimum(z, SP_CLAMP))) * LOG2E)
            if masked:
                sp = jnp.where(diag_mask, sp, 0.0)
            hi, lo = _split_bf16(sp)
            cs = (jnp.dot(a_mat, hi, preferred_element_type=F32)
                  + jnp.dot(a_mat, lo, preferred_element_type=F32))
            x = z + cs
            x_ref[s] = jnp.where(diag_mask, x, -jnp.inf) if masked else x
            carry = carry_ref[s]
            carry_ref[s] = carry + cs[0:1, :]
            sc_ref[s] = jnp.exp2(carry)
        live = (jnp.max(carry_ref[...]) > DEAD_LOG2).astype(jnp.int32)
        for s in range(SB_HEADS):
            w = jnp.exp2(x_ref[s]).astype(BF16)
            pv = jnp.dot(values(j, s), w, preferred_element_type=F32)
            acc_ref[s] += pv * sc_ref[s]
        return live

    acc_ref[...] = jnp.zeros_like(acc_ref)
    carry_ref[...] = jnp.zeros_like(carry_ref)
    for h in range(SB_HEADS):
        z_ref[h] = logits(j_diag, h)
    live0 = step(j_diag, True)

    def cond(state):
        n, live = state
        return jnp.logical_and(n < j_diag, live > 0)

    def body(state):
        n, _ = state
        return n + 1, step(j_diag - 1 - n, False)

    lax.while_loop(cond, body, (jnp.int32(0), live0))

    outs = []
    for g in range(N_GROUPS):
        sel = jnp.concatenate([acc_ref[g * HEADS_PER_GROUP + hh] for hh in range(HEADS_PER_GROUP)],
                              axis=0)
        outs.append(sel.T)
    o_ref[0] = jnp.concatenate(outs, axis=1).astype(o_ref.dtype)


def _cumsum_matrix():
    r = np.arange(K_TILE)
    return jnp.asarray(np.where(r[None, :] >= r[:, None], -1.0, 0.0), dtype=BF16)


def _attention(q, k, vt):
    B, S, _ = q.shape
    n_kt = S // K_TILE
    return pl.pallas_call(
        _attn_kernel,
        grid=(B, S // Q_TILE),
        in_specs=[
            pl.BlockSpec((1, Q_TILE, SB_WIDTH), lambda b, i: (b, i, 0)),
            pl.BlockSpec((1, S, SB_WIDTH), lambda b, i: (b, 0, 0)),
            pl.BlockSpec((1, n_kt, SB_WIDTH, K_TILE), lambda b, i: (b, 0, 0, 0)),
            pl.BlockSpec((K_TILE, K_TILE), lambda b, i: (0, 0)),
        ],
        out_specs=pl.BlockSpec((1, Q_TILE, SB_WIDTH), lambda b, i: (b, i, 0)),
        out_shape=jax.ShapeDtypeStruct((B, S, SB_WIDTH), BF16),
        scratch_shapes=[
            pltpu.VMEM((SB_HEADS, SB_HEAD_DIM, Q_TILE), F32),
            pltpu.VMEM((SB_HEADS, 1, Q_TILE), F32),
            pltpu.VMEM((SB_HEADS, K_TILE, Q_TILE), F32),
            pltpu.VMEM((SB_HEADS, K_TILE, Q_TILE), F32),
            pltpu.VMEM((SB_HEADS, 1, Q_TILE), F32),
        ],
        compiler_params=pltpu.CompilerParams(
            dimension_semantics=("arbitrary", "arbitrary"), vmem_limit_bytes=VMEM_LIMIT),
        name="attn",
    )(q, k, vt, _cumsum_matrix())


def _mix_kernel(attn_ref, u_ref, uh_ref, x_ref, mod_ref, wp_ref, ps_ref, wo_ref, g_ref,
                o_ref, ext_ref):
    t = pl.program_id(1)
    tm = TOKEN_TILE
    u = u_ref[0]
    ext_ref[0:POOL_HALO, :] = jnp.where(t > 0, uh_ref[0], 0.0)
    ext_ref[POOL_HALO:, :] = u
    row = t * tm + lax.broadcasted_iota(jnp.int32, (tm, 1), 0)
    ys = []
    for g, win in enumerate(POOL_WINDOWS):
        cols = slice(g * POOL_GROUP_DIM, (g + 1) * POOL_GROUP_DIM)
        s = u[:, cols]
        for d in range(1, win):
            s = s + ext_ref[POOL_HALO - d:POOL_HALO - d + tm, cols]
        inv_cnt = 1.0 / jnp.minimum(row + 1, win).astype(F32)
        pooled = s * inv_cnt - u[:, cols]
        y = jnp.dot(pooled.astype(BF16), wp_ref[g], preferred_element_type=F32) * ps_ref[:, cols]
        ys.append(y.astype(BF16))
    mix = (jnp.dot(attn_ref[0], wo_ref[0:SB_WIDTH, :], preferred_element_type=F32)
           + jnp.dot(jnp.concatenate(ys, axis=1), wo_ref[SB_WIDTH:, :], preferred_element_type=F32))
    gate = mod_ref[0, :, 2 * D_MODEL:3 * D_MODEL]
    o_ref[0] = x_ref[0] + gate * _rms(mix, g_ref[...])


def _mix(attn, u, x, mod3, w_pool_bf, pool_scale, w_out_bf, g_post):
    B, S, D = x.shape
    tm = TOKEN_TILE
    halo_blocks = tm // POOL_HALO
    return pl.pallas_call(
        _mix_kernel,
        grid=(B, S // tm),
        in_specs=[
            pl.BlockSpec((1, tm, SB_WIDTH), lambda b, t: (b, t, 0)),
            pl.BlockSpec((1, tm, POOL_WIDTH), lambda b, t: (b, t, 0)),
            pl.BlockSpec((1, POOL_HALO, POOL_WIDTH),
                         lambda b, t: (b, jnp.maximum(t * halo_blocks - 1, 0), 0)),
            pl.BlockSpec((1, tm, D), lambda b, t: (b, t, 0)),
            pl.BlockSpec((1, 1, N_MOD * D), lambda b, t: (b, 0, 0)),
            pl.BlockSpec(w_pool_bf.shape, lambda b, t: (0, 0, 0)),
            pl.BlockSpec((1, POOL_WIDTH), lambda b, t: (0, 0)),
            pl.BlockSpec(w_out_bf.shape, lambda b, t: (0, 0)),
            pl.BlockSpec((1, D), lambda b, t: (0, 0)),
        ],
        out_specs=pl.BlockSpec((1, tm, D), lambda b, t: (b, t, 0)),
        out_shape=jax.ShapeDtypeStruct((B, S, D), F32),
        scratch_shapes=[pltpu.VMEM((tm + POOL_HALO, POOL_WIDTH), F32)],
        compiler_params=pltpu.CompilerParams(
            dimension_semantics=("arbitrary", "arbitrary"), vmem_limit_bytes=VMEM_LIMIT),
        name="mix",
    )(attn, u, u, x, mod3, w_pool_bf, pool_scale, w_out_bf, g_post)


def _ffn_kernel(x_ref, mod_ref, gpre_ref, gpost_ref, wg_ref, wu_ref, wd_ref, o_ref):
    x = x_ref[0]
    shift = mod_ref[0, :, 3 * D_MODEL:4 * D_MODEL]
    scale = mod_ref[0, :, 4 * D_MODEL:5 * D_MODEL]
    gate = mod_ref[0, :, 5 * D_MODEL:6 * D_MODEL]
    h = (_rms(x, gpre_ref[...]) * (1.0 + scale) + shift).astype(BF16)
    f = jnp.zeros((TOKEN_TILE, D_MODEL), F32)
    for c in range(0, D_FF, FF_CHUNK):
        a = jnp.dot(h, wg_ref[:, c:c + FF_CHUNK], preferred_element_type=F32)
        b = jnp.dot(h, wu_ref[:, c:c + FF_CHUNK], preferred_element_type=F32)
        act = (a * jax.nn.sigmoid(a) * b).astype(BF16)
        f += jnp.dot(act, wd_ref[c:c + FF_CHUNK, :], preferred_element_type=F32)
    o_ref[0] = x + gate * _rms(f, gpost_ref[...])


def _ffn(x1, mod3, g_pre, g_post, wg_bf, wu_bf, wd_bf):
    B, S, D = x1.shape
    tm = TOKEN_TILE
    resident = functools.partial(pl.BlockSpec, pipeline_mode=pl.Buffered(1))
    return pl.pallas_call(
        _ffn_kernel,
        grid=(B, S // tm),
        in_specs=[
            pl.BlockSpec((1, tm, D), lambda b, t: (b, t, 0)),
            pl.BlockSpec((1, 1, N_MOD * D), lambda b, t: (b, 0, 0)),
            pl.BlockSpec((1, D), lambda b, t: (0, 0)),
            pl.BlockSpec((1, D), lambda b, t: (0, 0)),
            resident(wg_bf.shape, lambda b, t: (0, 0)),
            resident(wu_bf.shape, lambda b, t: (0, 0)),
            resident(wd_bf.shape, lambda b, t: (0, 0)),
        ],
        out_specs=pl.BlockSpec((1, tm, D), lambda b, t: (b, t, 0)),
        out_shape=jax.ShapeDtypeStruct((B, S, D), F32),
        compiler_params=pltpu.CompilerParams(
            dimension_semantics=("arbitrary", "arbitrary"), vmem_limit_bytes=VMEM_LIMIT),
        name="ffn",
    )(x1, mod3, g_pre, g_post, wg_bf, wu_bf, wd_bf)


def kernel(x, c, w_cond, b_cond, g_mix_pre, g_mix_post, w_in, w_pool, pool_scale, w_out,
           g_ffn_pre, g_ffn_post, w_gate, w_up, w_down):
    B, S, D = x.shape
    depth = w_cond.shape[0]
    for l in range(depth):
        mod3 = _cond(c, w_cond[l], b_cond[l]).reshape(B, 1, N_MOD * D)
        q, k, vt, u = _proj(x, mod3, g_mix_pre[l].reshape(1, D), w_in[l].astype(BF16))
        attn = _attention(q, k, vt)
        x = _mix(attn, u, x, mod3, w_pool[l].astype(BF16), pool_scale[l].reshape(1, POOL_WIDTH),
                 w_out[l].astype(BF16), g_mix_post[l].reshape(1, D))
        x = _ffn(x, mod3, g_ffn_pre[l].reshape(1, D), g_ffn_post[l].reshape(1, D),
                 w_gate[l].astype(BF16), w_up[l].astype(BF16), w_down[l].astype(BF16))
    return x
```

```python
import functools
import math

import jax
import jax.numpy as jnp
import numpy as np
from jax import lax
from jax.experimental import pallas as pl
from jax.experimental.pallas import tpu as pltpu

D_MODEL = 1024
SB_WIDTH = 512
SB_HEADS = 8
SB_HEAD_DIM = 64
POOL_WIDTH = 512
POOL_WINDOWS = (2, 4, 8, 16)
POOL_GROUP_DIM = 128
D_FF = 2816
N_MOD = 6
EPS = 1e-6

LANES = 128
Q_TILE = 256
K_TILE = 256
LOG2E = math.log2(math.e)
SP_CLAMP = 100.0
DEAD_LOG2 = -151.0
HEADS_PER_GROUP = LANES // SB_HEAD_DIM
N_GROUPS = SB_HEADS // HEADS_PER_GROUP
TOKEN_TILE = 512
POOL_HALO = 16
FF_CHUNK = 256
VMEM_LIMIT = 56 * 1024 * 1024
_ROW_HALVES = (slice(0, TOKEN_TILE // 2), slice(TOKEN_TILE // 2, TOKEN_TILE))

F32 = jnp.float32
BF16 = jnp.bfloat16


def _split_bf16(a):
    hi = a.astype(BF16)
    lo = (a - hi.astype(F32)).astype(BF16)
    return hi, lo


def _rms(x, g):
    return x * lax.rsqrt(jnp.mean(x * x, axis=-1, keepdims=True) + EPS) * g


def _cond_kernel(c_ref, w_ref, b_ref, o_ref):
    c = c_ref[...]
    a = c * jax.nn.sigmoid(c)
    a_hi, a_lo = _split_bf16(a)
    w_hi, w_lo = _split_bf16(w_ref[...])
    acc = jnp.dot(a_hi, w_hi, preferred_element_type=F32)
    acc += jnp.dot(a_lo, w_hi, preferred_element_type=F32)
    acc += jnp.dot(a_hi, w_lo, preferred_element_type=F32)
    o_ref[...] = acc + b_ref[...]


def _cond(c, w_cond, b_cond):
    B, D = c.shape
    N = w_cond.shape[1]
    tn = 1024
    return pl.pallas_call(
        _cond_kernel,
        grid=(N // tn,),
        in_specs=[
            pl.BlockSpec((B, D), lambda n: (0, 0)),
            pl.BlockSpec((D, tn), lambda n: (0, n)),
            pl.BlockSpec((1, tn), lambda n: (0, n)),
        ],
        out_specs=pl.BlockSpec((B, tn), lambda n: (0, n)),
        out_shape=jax.ShapeDtypeStruct((B, N), F32),
        compiler_params=pltpu.CompilerParams(
            dimension_semantics=("arbitrary",), vmem_limit_bytes=VMEM_LIMIT),
        name="cond",
    )(c, w_cond, b_cond.reshape(1, N))


def _proj_kernel(x_ref, mod_ref, g_ref, w_ref, q_ref, k_ref, vt_ref, u_ref):
    shift = mod_ref[0, :, 0:D_MODEL]
    scale = mod_ref[0, :, D_MODEL:2 * D_MODEL]
    q_scale = LOG2E / math.sqrt(SB_HEAD_DIM)
    hs = []
    for rows in _ROW_HALVES:
        hh = (_rms(x_ref[0, rows, :], g_ref[...]) * (1.0 + scale) + shift).astype(BF16)
        q = jnp.dot(hh, w_ref[:, 0:SB_WIDTH], preferred_element_type=F32)
        q_ref[0, rows, :] = (q * q_scale).astype(BF16)
        hs.append(hh)
    h = jnp.concatenate(hs, axis=0)
    k = jnp.dot(h, w_ref[:, SB_WIDTH:2 * SB_WIDTH], preferred_element_type=F32)
    k_ref[0] = k.astype(BF16)
    v = jnp.dot(h, w_ref[:, 2 * SB_WIDTH:3 * SB_WIDTH], preferred_element_type=F32)
    for t in range(TOKEN_TILE // K_TILE):
        vt_ref[0, t] = v[t * K_TILE:(t + 1) * K_TILE, :].T.astype(BF16)
    u_ref[0] = jnp.dot(h, w_ref[:, 3 * SB_WIDTH:], preferred_element_type=F32)


def _proj(x, mod3, g_pre, w_in_bf):
    B, S, D = x.shape
    tm = TOKEN_TILE
    n_kt = S // K_TILE
    return pl.pallas_call(
        _proj_kernel,
        grid=(B, S // tm),
        in_specs=[
            pl.BlockSpec((1, tm, D), lambda b, t: (b, t, 0)),
            pl.BlockSpec((1, 1, N_MOD * D), lambda b, t: (b, 0, 0)),
            pl.BlockSpec((1, D), lambda b, t: (0, 0)),
            pl.BlockSpec(w_in_bf.shape, lambda b, t: (0, 0)),
        ],
        out_specs=[
            pl.BlockSpec((1, tm, SB_WIDTH), lambda b, t: (b, t, 0)),
            pl.BlockSpec((1, tm, SB_WIDTH), lambda b, t: (b, t, 0)),
            pl.BlockSpec((1, tm // K_TILE, SB_WIDTH, K_TILE), lambda b, t: (b, t, 0, 0)),
            pl.BlockSpec((1, tm, POOL_WIDTH), lambda b, t: (b, t, 0)),
        ],
        out_shape=[
            jax.ShapeDtypeStruct((B, S, SB_WIDTH), BF16),
            jax.ShapeDtypeStruct((B, S, SB_WIDTH), BF16),
            jax.ShapeDtypeStruct((B, n_kt, SB_WIDTH, K_TILE), BF16),
            jax.ShapeDtypeStruct((B, S, POOL_WIDTH), F32),
        ],
        compiler_params=pltpu.CompilerParams(
            dimension_semantics=("arbitrary", "arbitrary"), vmem_limit_bytes=VMEM_LIMIT),
        name="proj",
    )(x, mod3, g_pre, w_in_bf)


def _attn_kernel(q_ref, k_ref, vt_ref, a_ref, o_ref, acc_ref, carry_ref, z_ref, x_ref, sc_ref):
    j_diag = pl.program_id(1)
    q = q_ref[0]
    lane = lax.broadcasted_iota(jnp.int32, (Q_TILE, LANES), 1)
    zero = jnp.zeros((), BF16)
    qh = []
    for h in range(SB_HEADS):
        g, hh = divmod(h, HEADS_PER_GROUP)
        qg = q[:, g * LANES:(g + 1) * LANES]
        qh.append(jnp.where((lane // SB_HEAD_DIM) == hh, qg, zero))

    a_mat = a_ref[...]
    diag_mask = (lax.broadcasted_iota(jnp.int32, (K_TILE, Q_TILE), 0)
                 < lax.broadcasted_iota(jnp.int32, (K_TILE, Q_TILE), 1))

    def logits(j, h):
        row0 = pl.multiple_of(j * K_TILE, K_TILE)
        g = h // HEADS_PER_GROUP
        kj = k_ref[0, pl.ds(row0, K_TILE), g * LANES:(g + 1) * LANES]
        return lax.dot_general(kj, qh[h], (((1,), (1,)), ((), ())), preferred_element_type=F32)

    def values(j, h):
        return vt_ref[0, j, h * SB_HEAD_DIM:(h + 1) * SB_HEAD_DIM, :]

    def step(j, masked):
        j_next = jnp.maximum(j - 1, 0)
        for s in range(SB_HEADS):
            z = z_ref[s]
            z_ref[s] = logits(j_next, s)
            sp = jnp.maximum(z, jnp.log(1.0 + jnp.exp2(jnp.minimum(z, SP_CLAMP))) * LOG2E)
            if masked:
                sp = jnp.where(diag_mask, sp, 0.0)
            cs = jnp.dot(a_mat, sp.astype(BF16), preferred_element_type=F32)
            x = z + cs
            x_ref[s] = jnp.where(diag_mask, x, -jnp.inf) if masked else x
            carry = carry_ref[s]
            carry_ref[s] = carry + cs[0:1, :]
            sc_ref[s] = jnp.exp2(carry)
        live = (jnp.max(carry_ref[...]) > DEAD_LOG2).astype(jnp.int32)
        for s in range(SB_HEADS):
            w = jnp.exp2(x_ref[s]).astype(BF16)
            pv = jnp.dot(values(j, s), w, preferred_element_type=F32)
            acc_ref[s] += pv * sc_ref[s]
        return live

    acc_ref[...] = jnp.zeros_like(acc_ref)
    carry_ref[...] = jnp.zeros_like(carry_ref)
    for h in range(SB_HEADS):
        z_ref[h] = logits(j_diag, h)
    live0 = step(j_diag, True)

    def cond(state):
        n, live = state
        return jnp.logical_and(n < j_diag, live > 0)

    def body(state):
        n, _ = state
        return n + 1, step(j_diag - 1 - n, False)

    lax.while_loop(cond, body, (jnp.int32(0), live0))

    outs = []
    for g in range(N_GROUPS):
        sel = jnp.concatenate([acc_ref[g * HEADS_PER_GROUP + hh] for hh in range(HEADS_PER_GROUP)],
                              axis=0)
        outs.append(sel.T)
    o_ref[0] = jnp.concatenate(outs, axis=1).astype(o_ref.dtype)


def _cumsum_matrix():
    r = np.arange(K_TILE)
    return jnp.asarray(np.where(r[None, :] >= r[:, None], -1.0, 0.0), dtype=BF16)


def _attention(q, k, vt):
    B, S, _ = q.shape
    n_kt = S // K_TILE
    return pl.pallas_call(
        _attn_kernel,
        grid=(B, S // Q_TILE),
        in_specs=[
            pl.BlockSpec((1, Q_TILE, SB_WIDTH), lambda b, i: (b, i, 0)),
            pl.BlockSpec((1, S, SB_WIDTH), lambda b, i: (b, 0, 0)),
            pl.BlockSpec((1, n_kt, SB_WIDTH, K_TILE), lambda b, i: (b, 0, 0, 0)),
            pl.BlockSpec((K_TILE, K_TILE), lambda b, i: (0, 0)),
        ],
        out_specs=pl.BlockSpec((1, Q_TILE, SB_WIDTH), lambda b, i: (b, i, 0)),
        out_shape=jax.ShapeDtypeStruct((B, S, SB_WIDTH), BF16),
        scratch_shapes=[
            pltpu.VMEM((SB_HEADS, SB_HEAD_DIM, Q_TILE), F32),
            pltpu.VMEM((SB_HEADS, 1, Q_TILE), F32),
            pltpu.VMEM((SB_HEADS, K_TILE, Q_TILE), F32),
            pltpu.VMEM((SB_HEADS, K_TILE, Q_TILE), F32),
            pltpu.VMEM((SB_HEADS, 1, Q_TILE), F32),
        ],
        compiler_params=pltpu.CompilerParams(
            dimension_semantics=("arbitrary", "arbitrary"), vmem_limit_bytes=VMEM_LIMIT),
        name="attn",
    )(q, k, vt, _cumsum_matrix())


def _mix_kernel(attn_ref, u_ref, uh_ref, x_ref, mod_ref, wp_ref, ps_ref, wo_ref, g_ref,
                o_ref, mix_ref):
    t = pl.program_id(1)
    tm = TOKEN_TILE
    u = u_ref[0]
    halo = jnp.where(t > 0, uh_ref[0], 0.0)
    row = t * tm + lax.broadcasted_iota(jnp.int32, (tm, 1), 0)
    ys = []
    n_chunk = D_MODEL // len(POOL_WINDOWS)
    for g, win in enumerate(POOL_WINDOWS):
        ocols = slice(g * n_chunk, (g + 1) * n_chunk)
        mix_ref[:, ocols] = jnp.dot(attn_ref[0], wo_ref[0:SB_WIDTH, ocols],
                                    preferred_element_type=F32)
        cols = slice(g * POOL_GROUP_DIM, (g + 1) * POOL_GROUP_DIM)
        s = jnp.concatenate([halo[:, cols], u[:, cols]], axis=0)
        w = 1
        while w < win:
            s = s + pltpu.roll(s, w, axis=0)
            w *= 2
        inv_cnt = 1.0 / jnp.minimum(row + 1, win).astype(F32)
        pooled = s[POOL_HALO:, :] * inv_cnt - u[:, cols]
        y = jnp.dot(pooled.astype(BF16), wp_ref[g], preferred_element_type=F32) * ps_ref[:, cols]
        ys.append(y.astype(BF16))
    ycat = jnp.concatenate(ys, axis=1)
    gate = mod_ref[0, :, 2 * D_MODEL:3 * D_MODEL]
    for half in range(2):
        rows = slice(half * (tm // 2), (half + 1) * (tm // 2))
        mix = mix_ref[rows, :] + jnp.dot(ycat[rows, :], wo_ref[SB_WIDTH:, :],
                                         preferred_element_type=F32)
        o_ref[0, rows, :] = x_ref[0, rows, :] + gate * _rms(mix, g_ref[...])


def _mix(attn, u, x, mod3, w_pool_bf, pool_scale, w_out_bf, g_post):
    B, S, D = x.shape
    tm = TOKEN_TILE
    halo_blocks = tm // POOL_HALO
    return pl.pallas_call(
        _mix_kernel,
        grid=(B, S // tm),
        in_specs=[
            pl.BlockSpec((1, tm, SB_WIDTH), lambda b, t: (b, t, 0)),
            pl.BlockSpec((1, tm, POOL_WIDTH), lambda b, t: (b, t, 0)),
            pl.BlockSpec((1, POOL_HALO, POOL_WIDTH),
                         lambda b, t: (b, jnp.maximum(t * halo_blocks - 1, 0), 0)),
            pl.BlockSpec((1, tm, D), lambda b, t: (b, t, 0)),
            pl.BlockSpec((1, 1, N_MOD * D), lambda b, t: (b, 0, 0)),
            pl.BlockSpec(w_pool_bf.shape, lambda b, t: (0, 0, 0)),
            pl.BlockSpec((1, POOL_WIDTH), lambda b, t: (0, 0)),
            pl.BlockSpec(w_out_bf.shape, lambda b, t: (0, 0)),
            pl.BlockSpec((1, D), lambda b, t: (0, 0)),
        ],
        out_specs=pl.BlockSpec((1, tm, D), lambda b, t: (b, t, 0)),
        out_shape=jax.ShapeDtypeStruct((B, S, D), F32),
        scratch_shapes=[pltpu.VMEM((tm, D), F32)],
        compiler_params=pltpu.CompilerParams(
            dimension_semantics=("arbitrary", "arbitrary"), vmem_limit_bytes=VMEM_LIMIT),
        name="mix",
    )(attn, u, u, x, mod3, w_pool_bf, pool_scale, w_out_bf, g_post)


def _ffn_kernel(x_ref, mod_ref, gpre_ref, gpost_ref, wg_ref, wu_ref, wd_ref, o_ref):
    shift = mod_ref[0, :, 3 * D_MODEL:4 * D_MODEL]
    scale = mod_ref[0, :, 4 * D_MODEL:5 * D_MODEL]
    gate = mod_ref[0, :, 5 * D_MODEL:6 * D_MODEL]

    def swiglu(hh, c):
        a = jnp.dot(hh, wg_ref[:, c:c + FF_CHUNK], preferred_element_type=F32)
        b = jnp.dot(hh, wu_ref[:, c:c + FF_CHUNK], preferred_element_type=F32)
        return (a * jax.nn.sigmoid(a) * b).astype(BF16)

    hs, acts = [], []
    for rows in _ROW_HALVES:
        hh = (_rms(x_ref[0, rows, :], gpre_ref[...]) * (1.0 + scale) + shift).astype(BF16)
        acts.append(swiglu(hh, 0))
        hs.append(hh)
    h = jnp.concatenate(hs, axis=0)
    f = jnp.dot(jnp.concatenate(acts, axis=0), wd_ref[0:FF_CHUNK, :], preferred_element_type=F32)
    last = D_FF - FF_CHUNK
    for c in range(FF_CHUNK, last, FF_CHUNK):
        f += jnp.dot(swiglu(h, c), wd_ref[c:c + FF_CHUNK, :], preferred_element_type=F32)
    act = swiglu(h, last)
    for rows in _ROW_HALVES:
        fh = f[rows, :] + jnp.dot(act[rows, :], wd_ref[last:, :], preferred_element_type=F32)
        o_ref[0, rows, :] = x_ref[0, rows, :] + gate * _rms(fh, gpost_ref[...])


def _ffn(x1, mod3, g_pre, g_post, wg_bf, wu_bf, wd_bf):
    B, S, D = x1.shape
    tm = TOKEN_TILE
    resident = functools.partial(pl.BlockSpec, pipeline_mode=pl.Buffered(1))
    return pl.pallas_call(
        _ffn_kernel,
        grid=(B, S // tm),
        in_specs=[
            pl.BlockSpec((1, tm, D), lambda b, t: (b, t, 0)),
            pl.BlockSpec((1, 1, N_MOD * D), lambda b, t: (b, 0, 0)),
            pl.BlockSpec((1, D), lambda b, t: (0, 0)),
            pl.BlockSpec((1, D), lambda b, t: (0, 0)),
            resident(wg_bf.shape, lambda b, t: (0, 0)),
            resident(wu_bf.shape, lambda b, t: (0, 0)),
            resident(wd_bf.shape, lambda b, t: (0, 0)),
        ],
        out_specs=pl.BlockSpec((1, tm, D), lambda b, t: (b, t, 0)),
        out_shape=jax.ShapeDtypeStruct((B, S, D), F32),
        compiler_params=pltpu.CompilerParams(
            dimension_semantics=("arbitrary", "arbitrary"), vmem_limit_bytes=VMEM_LIMIT),
        name="ffn",
    )(x1, mod3, g_pre, g_post, wg_bf, wu_bf, wd_bf)


def kernel(x, c, w_cond, b_cond, g_mix_pre, g_mix_post, w_in, w_pool, pool_scale, w_out,
           g_ffn_pre, g_ffn_post, w_gate, w_up, w_down):
    B, S, D = x.shape
    depth = w_cond.shape[0]
    for l in range(depth):
        mod3 = _cond(c, w_cond[l], b_cond[l]).reshape(B, 1, N_MOD * D)
        q, k, vt, u = _proj(x, mod3, g_mix_pre[l].reshape(1, D), w_in[l].astype(BF16))
        attn = _attention(q, k, vt)
        x = _mix(attn, u, x, mod3, w_pool[l].astype(BF16), pool_scale[l].reshape(1, POOL_WIDTH),
                 w_out[l].astype(BF16), g_mix_post[l].reshape(1, D))
        x = _ffn(x, mod3, g_ffn_pre[l].reshape(1, D), g_ffn_post[l].reshape(1, D),
                 w_gate[l].astype(BF16), w_up[l].astype(BF16), w_down[l].astype(BF16))
    return x
```

```python
import functools
import math

import jax
import jax.numpy as jnp
import numpy as np
from jax import lax
from jax.experimental import pallas as pl
from jax.experimental.pallas import tpu as pltpu

D_MODEL = 1024
SB_WIDTH = 512
SB_HEADS = 8
SB_HEAD_DIM = 64
POOL_WIDTH = 512
POOL_WINDOWS = (2, 4, 8, 16)
POOL_GROUP_DIM = 128
D_FF = 2816
N_MOD = 6
EPS = 1e-6

LANES = 128
Q_TILE = 256
K_TILE = 256
LOG2E = math.log2(math.e)
SP_CLAMP = 100.0
DEAD_LOG2 = -151.0
HEADS_PER_GROUP = LANES // SB_HEAD_DIM
N_GROUPS = SB_HEADS // HEADS_PER_GROUP
TOKEN_TILE = 512
POOL_HALO = 16
FF_CHUNK = 256
VMEM_LIMIT = 56 * 1024 * 1024
_ROW_HALVES = (slice(0, TOKEN_TILE // 2), slice(TOKEN_TILE // 2, TOKEN_TILE))

F32 = jnp.float32
BF16 = jnp.bfloat16


def _split_bf16(a):
    hi = a.astype(BF16)
    lo = (a - hi.astype(F32)).astype(BF16)
    return hi, lo


def _rms(x, g):
    return x * lax.rsqrt(jnp.mean(x * x, axis=-1, keepdims=True) + EPS) * g


def _cond_kernel(c_ref, w_ref, b_ref, o_ref):
    c = c_ref[...]
    a = c * jax.nn.sigmoid(c)
    a_hi, a_lo = _split_bf16(a)
    w_hi, w_lo = _split_bf16(w_ref[...])
    acc = jnp.dot(a_hi, w_hi, preferred_element_type=F32)
    acc += jnp.dot(a_lo, w_hi, preferred_element_type=F32)
    acc += jnp.dot(a_hi, w_lo, preferred_element_type=F32)
    o_ref[...] = acc + b_ref[...]


def _cond(c, w_cond, b_cond):
    B, D = c.shape
    N = w_cond.shape[1]
    tn = 1024
    return pl.pallas_call(
        _cond_kernel,
        grid=(N // tn,),
        in_specs=[
            pl.BlockSpec((B, D), lambda n: (0, 0)),
            pl.BlockSpec((D, tn), lambda n: (0, n)),
            pl.BlockSpec((1, tn), lambda n: (0, n)),
        ],
        out_specs=pl.BlockSpec((B, tn), lambda n: (0, n)),
        out_shape=jax.ShapeDtypeStruct((B, N), F32),
        compiler_params=pltpu.CompilerParams(
            dimension_semantics=("arbitrary",), vmem_limit_bytes=VMEM_LIMIT),
        name="cond",
    )(c, w_cond, b_cond.reshape(1, N))


def _proj_kernel(x_ref, mod_ref, g_ref, w_ref, q_ref, k_ref, vt_ref, u_ref):
    shift = mod_ref[0, :, 0:D_MODEL]
    scale = mod_ref[0, :, D_MODEL:2 * D_MODEL]
    h = (_rms(x_ref[0], g_ref[...]) * (1.0 + scale) + shift).astype(BF16)
    q = jnp.dot(h, w_ref[:, 0:SB_WIDTH], preferred_element_type=F32)
    q_ref[0] = (q * (LOG2E / math.sqrt(SB_HEAD_DIM))).astype(BF16)
    k = jnp.dot(h, w_ref[:, SB_WIDTH:2 * SB_WIDTH], preferred_element_type=F32)
    k_ref[0] = k.astype(BF16)
    v = jnp.dot(h, w_ref[:, 2 * SB_WIDTH:3 * SB_WIDTH], preferred_element_type=F32)
    for t in range(TOKEN_TILE // K_TILE):
        vt_ref[0, t] = v[t * K_TILE:(t + 1) * K_TILE, :].T.astype(BF16)
    u_ref[0] = jnp.dot(h, w_ref[:, 3 * SB_WIDTH:], preferred_element_type=F32)


def _proj(x, mod3, g_pre, w_in_bf):
    B, S, D = x.shape
    tm = TOKEN_TILE
    n_kt = S // K_TILE
    return pl.pallas_call(
        _proj_kernel,
        grid=(B, S // tm),
        in_specs=[
            pl.BlockSpec((1, tm, D), lambda b, t: (b, t, 0)),
            pl.BlockSpec((1, 1, N_MOD * D), lambda b, t: (b, 0, 0)),
            pl.BlockSpec((1, D), lambda b, t: (0, 0)),
            pl.BlockSpec(w_in_bf.shape, lambda b, t: (0, 0)),
        ],
        out_specs=[
            pl.BlockSpec((1, tm, SB_WIDTH), lambda b, t: (b, t, 0)),
            pl.BlockSpec((1, tm, SB_WIDTH), lambda b, t: (b, t, 0)),
            pl.BlockSpec((1, tm // K_TILE, SB_WIDTH, K_TILE), lambda b, t: (b, t, 0, 0)),
            pl.BlockSpec((1, tm, POOL_WIDTH), lambda b, t: (b, t, 0)),
        ],
        out_shape=[
            jax.ShapeDtypeStruct((B, S, SB_WIDTH), BF16),
            jax.ShapeDtypeStruct((B, S, SB_WIDTH), BF16),
            jax.ShapeDtypeStruct((B, n_kt, SB_WIDTH, K_TILE), BF16),
            jax.ShapeDtypeStruct((B, S, POOL_WIDTH), F32),
        ],
        compiler_params=pltpu.CompilerParams(
            dimension_semantics=("arbitrary", "arbitrary"), vmem_limit_bytes=VMEM_LIMIT),
        name="proj",
    )(x, mod3, g_pre, w_in_bf)


def _attn_kernel(q_ref, k_ref, vt_ref, a_ref, o_ref, acc_ref, carry_ref, z_ref, x_ref, sc_ref):
    j_diag = pl.program_id(1)
    q = q_ref[0]
    lane = lax.broadcasted_iota(jnp.int32, (Q_TILE, LANES), 1)
    zero = jnp.zeros((), BF16)
    qh = []
    for h in range(SB_HEADS):
        g, hh = divmod(h, HEADS_PER_GROUP)
        qg = q[:, g * LANES:(g + 1) * LANES]
        qh.append(jnp.where((lane // SB_HEAD_DIM) == hh, qg, zero))

    a_mat = a_ref[...]
    diag_mask = (lax.broadcasted_iota(jnp.int32, (K_TILE, Q_TILE), 0)
                 < lax.broadcasted_iota(jnp.int32, (K_TILE, Q_TILE), 1))

    def logits(j, h):
        row0 = pl.multiple_of(j * K_TILE, K_TILE)
        g = h // HEADS_PER_GROUP
        kj = k_ref[0, pl.ds(row0, K_TILE), g * LANES:(g + 1) * LANES]
        return lax.dot_general(kj, qh[h], (((1,), (1,)), ((), ())), preferred_element_type=F32)

    def values(j, h):
        return vt_ref[0, j, h * SB_HEAD_DIM:(h + 1) * SB_HEAD_DIM, :]

    def step(j, masked):
        j_next = jnp.maximum(j - 1, 0)
        for s in range(SB_HEADS):
            z = z_ref[s]
            z_ref[s] = logits(j_next, s)
            sp = jnp.maximum(z, jnp.log(1.0 + jnp.exp2(jnp.minimum(z, SP_CLAMP))) * LOG2E)
            if masked:
                sp = jnp.where(diag_mask, sp, 0.0)
            cs = jnp.dot(a_mat, sp.astype(BF16), preferred_element_type=F32)
            x = z + cs
            x_ref[s] = jnp.where(diag_mask, x, -jnp.inf) if masked else x
            carry = carry_ref[s]
            carry_ref[s] = carry + cs[0:1, :]
            sc_ref[s] = jnp.exp2(carry)
        live = (jnp.max(carry_ref[...]) > DEAD_LOG2).astype(jnp.int32)
        for s in range(SB_HEADS):
            w = jnp.exp2(x_ref[s]).astype(BF16)
            pv = jnp.dot(values(j, s), w, preferred_element_type=F32)
            acc_ref[s] += pv * sc_ref[s]
        return live

    acc_ref[...] = jnp.zeros_like(acc_ref)
    carry_ref[...] = jnp.zeros_like(carry_ref)
    for h in range(SB_HEADS):
        z_ref[h] = logits(j_diag, h)
    live0 = step(j_diag, True)

    def cond(state):
        n, live = state
        return jnp.logical_and(n < j_diag, live > 0)

    def body(state):
        n, _ = state
        return n + 1, step(j_diag - 1 - n, False)

    lax.while_loop(cond, body, (jnp.int32(0), live0))

    outs = []
    for g in range(N_GROUPS):
        sel = jnp.concatenate([acc_ref[g * HEADS_PER_GROUP + hh] for hh in range(HEADS_PER_GROUP)],
                              axis=0)
        outs.append(sel.T)
    o_ref[0] = jnp.concatenate(outs, axis=1).astype(o_ref.dtype)


def _cumsum_matrix():
    r = np.arange(K_TILE)
    return jnp.asarray(np.where(r[None, :] >= r[:, None], -1.0, 0.0), dtype=BF16)


def _attention(q, k, vt):
    B, S, _ = q.shape
    n_kt = S // K_TILE
    return pl.pallas_call(
        _attn_kernel,
        grid=(B, S // Q_TILE),
        in_specs=[
            pl.BlockSpec((1, Q_TILE, SB_WIDTH), lambda b, i: (b, i, 0)),
            pl.BlockSpec((1, S, SB_WIDTH), lambda b, i: (b, 0, 0)),
            pl.BlockSpec((1, n_kt, SB_WIDTH, K_TILE), lambda b, i: (b, 0, 0, 0)),
            pl.BlockSpec((K_TILE, K_TILE), lambda b, i: (0, 0)),
        ],
        out_specs=pl.BlockSpec((1, Q_TILE, SB_WIDTH), lambda b, i: (b, i, 0)),
        out_shape=jax.ShapeDtypeStruct((B, S, SB_WIDTH), BF16),
        scratch_shapes=[
            pltpu.VMEM((SB_HEADS, SB_HEAD_DIM, Q_TILE), F32),
            pltpu.VMEM((SB_HEADS, 1, Q_TILE), F32),
            pltpu.VMEM((SB_HEADS, K_TILE, Q_TILE), F32),
            pltpu.VMEM((SB_HEADS, K_TILE, Q_TILE), F32),
            pltpu.VMEM((SB_HEADS, 1, Q_TILE), F32),
        ],
        compiler_params=pltpu.CompilerParams(
            dimension_semantics=("arbitrary", "arbitrary"), vmem_limit_bytes=VMEM_LIMIT),
        name="attn",
    )(q, k, vt, _cumsum_matrix())


def _tail_kernel(attn_ref, u_ref, uh_ref, x_ref, mod_ref, wp_ref, ps_ref, wo_ref, gmix_ref,
                 gpre_ref, gpost_ref, wg_ref, wu_ref, wd_ref, o_ref, mix_ref):
    t = pl.program_id(1)
    tm = TOKEN_TILE
    u = u_ref[0]
    halo = jnp.where(t > 0, uh_ref[0], 0.0)
    row = t * tm + lax.broadcasted_iota(jnp.int32, (tm, 1), 0)
    ys = []
    n_chunk = D_MODEL // len(POOL_WINDOWS)
    for g, win in enumerate(POOL_WINDOWS):
        ocols = slice(g * n_chunk, (g + 1) * n_chunk)
        mix_ref[:, ocols] = jnp.dot(attn_ref[0], wo_ref[0:SB_WIDTH, ocols],
                                    preferred_element_type=F32)
        cols = slice(g * POOL_GROUP_DIM, (g + 1) * POOL_GROUP_DIM)
        s = jnp.concatenate([halo[:, cols], u[:, cols]], axis=0)
        w = 1
        while w < win:
            s = s + pltpu.roll(s, w, axis=0)
            w *= 2
        inv_cnt = 1.0 / jnp.minimum(row + 1, win).astype(F32)
        pooled = s[POOL_HALO:, :] * inv_cnt - u[:, cols]
        y = jnp.dot(pooled.astype(BF16), wp_ref[g], preferred_element_type=F32) * ps_ref[:, cols]
        ys.append(y.astype(BF16))
    ycat = jnp.concatenate(ys, axis=1)

    gate_m = mod_ref[0, :, 2 * D_MODEL:3 * D_MODEL]
    shift_f = mod_ref[0, :, 3 * D_MODEL:4 * D_MODEL]
    scale_f = mod_ref[0, :, 4 * D_MODEL:5 * D_MODEL]
    gate_f = mod_ref[0, :, 5 * D_MODEL:6 * D_MODEL]

    def swiglu(hh, c):
        a = jnp.dot(hh, wg_ref[:, c:c + FF_CHUNK], preferred_element_type=F32)
        b = jnp.dot(hh, wu_ref[:, c:c + FF_CHUNK], preferred_element_type=F32)
        return (a * jax.nn.sigmoid(a) * b).astype(BF16)

    mixes = [mix_ref[rows, :] + jnp.dot(ycat[rows, :], wo_ref[SB_WIDTH:, :],
                                        preferred_element_type=F32) for rows in _ROW_HALVES]
    x1s, hs, acts = [], [], []
    for rows, mix in zip(_ROW_HALVES, mixes):
        x1 = x_ref[0, rows, :] + gate_m * _rms(mix, gmix_ref[...])
        hh = (_rms(x1, gpre_ref[...]) * (1.0 + scale_f) + shift_f).astype(BF16)
        acts.append(swiglu(hh, 0))
        x1s.append(x1)
        hs.append(hh)
    h = jnp.concatenate(hs, axis=0)
    f = jnp.dot(jnp.concatenate(acts, axis=0), wd_ref[0:FF_CHUNK, :], preferred_element_type=F32)
    last = D_FF - FF_CHUNK
    for c in range(FF_CHUNK, last, FF_CHUNK):
        f += jnp.dot(swiglu(h, c), wd_ref[c:c + FF_CHUNK, :], preferred_element_type=F32)
    for rows, x1, hh in zip(_ROW_HALVES, x1s, hs):
        fh = f[rows, :] + jnp.dot(swiglu(hh, last), wd_ref[last:, :], preferred_element_type=F32)
        o_ref[0, rows, :] = x1 + gate_f * _rms(fh, gpost_ref[...])


def _tail(attn, u, x, mod3, w_pool_bf, pool_scale, w_out_bf, g_mix_post, g_ffn_pre, g_ffn_post,
          wg_bf, wu_bf, wd_bf):
    B, S, D = x.shape
    tm = TOKEN_TILE
    halo_blocks = tm // POOL_HALO
    resident = functools.partial(pl.BlockSpec, pipeline_mode=pl.Buffered(1))
    vec = pl.BlockSpec((1, D), lambda b, t: (0, 0))
    return pl.pallas_call(
        _tail_kernel,
        grid=(B, S // tm),
        in_specs=[
            pl.BlockSpec((1, tm, SB_WIDTH), lambda b, t: (b, t, 0)),
            pl.BlockSpec((1, tm, POOL_WIDTH), lambda b, t: (b, t, 0)),
            pl.BlockSpec((1, POOL_HALO, POOL_WIDTH),
                         lambda b, t: (b, jnp.maximum(t * halo_blocks - 1, 0), 0)),
            pl.BlockSpec((1, tm, D), lambda b, t: (b, t, 0)),
            pl.BlockSpec((1, 1, N_MOD * D), lambda b, t: (b, 0, 0)),
            resident(w_pool_bf.shape, lambda b, t: (0, 0, 0)),
            pl.BlockSpec((1, POOL_WIDTH), lambda b, t: (0, 0)),
            resident(w_out_bf.shape, lambda b, t: (0, 0)),
            vec, vec, vec,
            resident(wg_bf.shape, lambda b, t: (0, 0)),
            resident(wu_bf.shape, lambda b, t: (0, 0)),
            resident(wd_bf.shape, lambda b, t: (0, 0)),
        ],
        out_specs=pl.BlockSpec((1, tm, D), lambda b, t: (b, t, 0)),
        out_shape=jax.ShapeDtypeStruct((B, S, D), F32),
        scratch_shapes=[pltpu.VMEM((tm, D), F32)],
        compiler_params=pltpu.CompilerParams(
            dimension_semantics=("arbitrary", "arbitrary"), vmem_limit_bytes=VMEM_LIMIT),
        name="tail",
    )(attn, u, u, x, mod3, w_pool_bf, pool_scale, w_out_bf, g_mix_post, g_ffn_pre, g_ffn_post,
      wg_bf, wu_bf, wd_bf)


def kernel(x, c, w_cond, b_cond, g_mix_pre, g_mix_post, w_in, w_pool, pool_scale, w_out,
           g_ffn_pre, g_ffn_post, w_gate, w_up, w_down):
    B, S, D = x.shape
    depth = w_cond.shape[0]
    for l in range(depth):
        mod3 = _cond(c, w_cond[l], b_cond[l]).reshape(B, 1, N_MOD * D)
        q, k, vt, u = _proj(x, mod3, g_mix_pre[l].reshape(1, D), w_in[l].astype(BF16))
        attn = _attention(q, k, vt)
        x = _tail(attn, u, x, mod3, w_pool[l].astype(BF16), pool_scale[l].reshape(1, POOL_WIDTH),
                  w_out[l].astype(BF16), g_mix_post[l].reshape(1, D), g_ffn_pre[l].reshape(1, D),
                  g_ffn_post[l].reshape(1, D), w_gate[l].astype(BF16), w_up[l].astype(BF16),
                  w_down[l].astype(BF16))
    return x
```

```python
import functools
import math

import jax
import jax.numpy as jnp
import numpy as np
from jax import lax
from jax.experimental import pallas as pl
from jax.experimental.pallas import tpu as pltpu

D_MODEL = 1024
SB_WIDTH = 512
SB_HEADS = 8
SB_HEAD_DIM = 64
POOL_WIDTH = 512
POOL_WINDOWS = (2, 4, 8, 16)
POOL_GROUP_DIM = 128
D_FF = 2816
N_MOD = 6
EPS = 1e-6

LANES = 128
Q_TILE = 256
K_TILE = 256
LOG2E = math.log2(math.e)
SP_CLAMP = 100.0
DEAD_LOG2 = -151.0
HEADS_PER_GROUP = LANES // SB_HEAD_DIM
N_GROUPS = SB_HEADS // HEADS_PER_GROUP
TOKEN_TILE = 512
POOL_HALO = 16
FF_CHUNK = 256
VMEM_LIMIT = 56 * 1024 * 1024
_ROW_HALVES = (slice(0, TOKEN_TILE // 2), slice(TOKEN_TILE // 2, TOKEN_TILE))

F32 = jnp.float32
BF16 = jnp.bfloat16


def _split_bf16(a):
    hi = a.astype(BF16)
    lo = (a - hi.astype(F32)).astype(BF16)
    return hi, lo


def _rms(x, a):
    return x * lax.rsqrt(jnp.mean(x * x, axis=-1, keepdims=True) + EPS) * a


def _cond_kernel(c_ref, w_ref, b_ref, o_ref):
    c = c_ref[...]
    a = c * jax.nn.sigmoid(c)
    a_hi, a_lo = _split_bf16(a)
    w_hi, w_lo = _split_bf16(w_ref[...])
    acc = jnp.dot(a_hi, w_hi, preferred_element_type=F32)
    acc += jnp.dot(a_lo, w_hi, preferred_element_type=F32)
    acc += jnp.dot(a_hi, w_lo, preferred_element_type=F32)
    o_ref[...] = acc + b_ref[...]


def _cond(c, w_cond, b_cond):
    B, D = c.shape
    N = w_cond.shape[1]
    tn = 1024
    return pl.pallas_call(
        _cond_kernel,
        grid=(N // tn,),
        in_specs=[
            pl.BlockSpec((B, D), lambda n: (0, 0)),
            pl.BlockSpec((D, tn), lambda n: (0, n)),
            pl.BlockSpec((1, tn), lambda n: (0, n)),
        ],
        out_specs=pl.BlockSpec((B, tn), lambda n: (0, n)),
        out_shape=jax.ShapeDtypeStruct((B, N), F32),
        compiler_params=pltpu.CompilerParams(
            dimension_semantics=("arbitrary",), vmem_limit_bytes=VMEM_LIMIT),
        name="cond",
    )(c, w_cond, b_cond.reshape(1, N))


def _proj_kernel(x_ref, mod_ref, g_ref, w_ref, q_ref, k_ref, vt_ref, u_ref):
    shift = mod_ref[0, :, 0:D_MODEL]
    scale = mod_ref[0, :, D_MODEL:2 * D_MODEL]
    h = (_rms(x_ref[0], g_ref[...] * (1.0 + scale)) + shift).astype(BF16)
    q = jnp.dot(h, w_ref[:, 0:SB_WIDTH], preferred_element_type=F32)
    q_ref[0] = (q * (LOG2E / math.sqrt(SB_HEAD_DIM))).astype(BF16)
    k = jnp.dot(h, w_ref[:, SB_WIDTH:2 * SB_WIDTH], preferred_element_type=F32)
    k_ref[0] = k.astype(BF16)
    v = jnp.dot(h, w_ref[:, 2 * SB_WIDTH:3 * SB_WIDTH], preferred_element_type=F32)
    for t in range(TOKEN_TILE // K_TILE):
        vt_ref[0, t] = v[t * K_TILE:(t + 1) * K_TILE, :].T.astype(BF16)
    u_ref[0] = jnp.dot(h, w_ref[:, 3 * SB_WIDTH:], preferred_element_type=F32)


def _proj(x, mod3, g_pre, w_in_bf):
    B, S, D = x.shape
    tm = TOKEN_TILE
    n_kt = S // K_TILE
    return pl.pallas_call(
        _proj_kernel,
        grid=(B, S // tm),
        in_specs=[
            pl.BlockSpec((1, tm, D), lambda b, t: (b, t, 0)),
            pl.BlockSpec((1, 1, N_MOD * D), lambda b, t: (b, 0, 0)),
            pl.BlockSpec((1, D), lambda b, t: (0, 0)),
            pl.BlockSpec(w_in_bf.shape, lambda b, t: (0, 0)),
        ],
        out_specs=[
            pl.BlockSpec((1, tm, SB_WIDTH), lambda b, t: (b, t, 0)),
            pl.BlockSpec((1, tm, SB_WIDTH), lambda b, t: (b, t, 0)),
            pl.BlockSpec((1, tm // K_TILE, SB_WIDTH, K_TILE), lambda b, t: (b, t, 0, 0)),
            pl.BlockSpec((1, tm, POOL_WIDTH), lambda b, t: (b, t, 0)),
        ],
        out_shape=[
            jax.ShapeDtypeStruct((B, S, SB_WIDTH), BF16),
            jax.ShapeDtypeStruct((B, S, SB_WIDTH), BF16),
            jax.ShapeDtypeStruct((B, n_kt, SB_WIDTH, K_TILE), BF16),
            jax.ShapeDtypeStruct((B, S, POOL_WIDTH), F32),
        ],
        compiler_params=pltpu.CompilerParams(
            dimension_semantics=("arbitrary", "arbitrary"), vmem_limit_bytes=VMEM_LIMIT),
        name="proj",
    )(x, mod3, g_pre, w_in_bf)


def _attn_kernel(q_ref, k_ref, vt_ref, a_ref, o_ref, acc_ref, carry_ref, z_ref, x_ref, sc_ref):
    j_diag = pl.program_id(1)
    q = q_ref[0]
    lane = lax.broadcasted_iota(jnp.int32, (Q_TILE, LANES), 1)
    zero = jnp.zeros((), BF16)
    qh = []
    for h in range(SB_HEADS):
        g, hh = divmod(h, HEADS_PER_GROUP)
        qg = q[:, g * LANES:(g + 1) * LANES]
        qh.append(jnp.where((lane // SB_HEAD_DIM) == hh, qg, zero))

    a_mat = a_ref[...]
    diag_mask = (lax.broadcasted_iota(jnp.int32, (K_TILE, Q_TILE), 0)
                 < lax.broadcasted_iota(jnp.int32, (K_TILE, Q_TILE), 1))

    def logits(j, h):
        row0 = pl.multiple_of(j * K_TILE, K_TILE)
        g = h // HEADS_PER_GROUP
        kj = k_ref[0, pl.ds(row0, K_TILE), g * LANES:(g + 1) * LANES]
        return lax.dot_general(kj, qh[h], (((1,), (1,)), ((), ())), preferred_element_type=F32)

    def softplus2(z):
        return jnp.maximum(z, jnp.log(1.0 + jnp.exp2(jnp.minimum(z, SP_CLAMP))) * LOG2E)

    def values(j, h):
        return vt_ref[0, j, h * SB_HEAD_DIM:(h + 1) * SB_HEAD_DIM, :]

    def step(j, diagonal):
        j_next = jnp.maximum(j - 1, 0)

        def weigh(s):
            w = jnp.exp2(x_ref[s]).astype(BF16)
            pv = jnp.dot(values(j, s), w, preferred_element_type=F32)
            acc_ref[s] += pv * sc_ref[s]

        for s in range(SB_HEADS):
            z = z_ref[s]
            z_ref[s] = logits(j_next, s)
            sp = softplus2(z)
            if diagonal:
                sp = jnp.where(diag_mask, sp, 0.0)
            cs = jnp.dot(a_mat, sp.astype(BF16), preferred_element_type=F32)
            x = z + cs
            x_ref[s] = jnp.where(diag_mask, x, -jnp.inf) if diagonal else x
            carry = carry_ref[s]
            carry_ref[s] = carry + cs[0:1, :]
            sc_ref[s] = jnp.exp2(carry)
        live = (jnp.max(carry_ref[...]) > DEAD_LOG2).astype(jnp.int32)
        for s in range(SB_HEADS):
            weigh(s)
        return live

    acc_ref[...] = jnp.zeros_like(acc_ref)
    carry_ref[...] = jnp.zeros_like(carry_ref)
    for h in range(SB_HEADS):
        z_ref[h] = logits(j_diag, h)
    live0 = step(j_diag, True)

    def cond(state):
        n, live = state
        return jnp.logical_and(n < j_diag, live > 0)

    def body(state):
        n, _ = state
        return n + 1, step(j_diag - 1 - n, False)

    lax.while_loop(cond, body, (jnp.int32(0), live0))

    outs = []
    for g in range(N_GROUPS):
        sel = jnp.concatenate([acc_ref[g * HEADS_PER_GROUP + hh] for hh in range(HEADS_PER_GROUP)],
                              axis=0)
        outs.append(sel.T)
    o_ref[0] = jnp.concatenate(outs, axis=1).astype(o_ref.dtype)


def _cumsum_matrix():
    r = np.arange(K_TILE)
    return jnp.asarray(np.where(r[None, :] >= r[:, None], -1.0, 0.0), dtype=BF16)


def _attention(q, k, vt):
    B, S, _ = q.shape
    n_kt = S // K_TILE
    return pl.pallas_call(
        _attn_kernel,
        grid=(B, S // Q_TILE),
        in_specs=[
            pl.BlockSpec((1, Q_TILE, SB_WIDTH), lambda b, i: (b, i, 0)),
            pl.BlockSpec((1, S, SB_WIDTH), lambda b, i: (b, 0, 0)),
            pl.BlockSpec((1, n_kt, SB_WIDTH, K_TILE), lambda b, i: (b, 0, 0, 0)),
            pl.BlockSpec((K_TILE, K_TILE), lambda b, i: (0, 0)),
        ],
        out_specs=pl.BlockSpec((1, Q_TILE, SB_WIDTH), lambda b, i: (b, i, 0)),
        out_shape=jax.ShapeDtypeStruct((B, S, SB_WIDTH), BF16),
        scratch_shapes=[
            pltpu.VMEM((SB_HEADS, SB_HEAD_DIM, Q_TILE), F32),
            pltpu.VMEM((SB_HEADS, 1, Q_TILE), F32),
            pltpu.VMEM((SB_HEADS, K_TILE, Q_TILE), F32),
            pltpu.VMEM((SB_HEADS, K_TILE, Q_TILE), F32),
            pltpu.VMEM((SB_HEADS, 1, Q_TILE), F32),
        ],
        compiler_params=pltpu.CompilerParams(
            dimension_semantics=("arbitrary", "arbitrary"), vmem_limit_bytes=VMEM_LIMIT),
        name="attn",
    )(q, k, vt, _cumsum_matrix())


def _tail_kernel(attn_ref, u_ref, uh_ref, x_ref, mod_ref, wp_ref, ps_ref, wo_ref, gmix_ref,
                 gpre_ref, gpost_ref, wg_ref, wu_ref, wd_ref, o_ref, mix_ref):
    t = pl.program_id(1)
    tm = TOKEN_TILE
    u = u_ref[0]
    halo = jnp.where(t > 0, uh_ref[0], 0.0)
    row = t * tm + lax.broadcasted_iota(jnp.int32, (tm, 1), 0)
    ys = []
    n_chunk = D_MODEL // len(POOL_WINDOWS)
    for g, win in enumerate(POOL_WINDOWS):
        ocols = slice(g * n_chunk, (g + 1) * n_chunk)
        mix_ref[:, ocols] = jnp.dot(attn_ref[0], wo_ref[0:SB_WIDTH, ocols],
                                    preferred_element_type=F32)
        cols = slice(g * POOL_GROUP_DIM, (g + 1) * POOL_GROUP_DIM)
        s = jnp.concatenate([halo[:, cols], u[:, cols]], axis=0)
        w = 1
        while w < win:
            s = s + pltpu.roll(s, w, axis=0)
            w *= 2
        inv_cnt = 1.0 / jnp.minimum(row + 1, win).astype(F32)
        pooled = s[POOL_HALO:, :] * inv_cnt - u[:, cols]
        y = jnp.dot(pooled.astype(BF16), wp_ref[g], preferred_element_type=F32) * ps_ref[:, cols]
        ys.append(y.astype(BF16))
    ycat = jnp.concatenate(ys, axis=1)

    shift_f = mod_ref[0, :, 3 * D_MODEL:4 * D_MODEL]
    a_mix = gmix_ref[...] * mod_ref[0, :, 2 * D_MODEL:3 * D_MODEL]
    a_pre = gpre_ref[...] * (1.0 + mod_ref[0, :, 4 * D_MODEL:5 * D_MODEL])
    a_post = gpost_ref[...] * mod_ref[0, :, 5 * D_MODEL:6 * D_MODEL]

    def swiglu(hh, c):
        a = jnp.dot(hh, wg_ref[:, c:c + FF_CHUNK], preferred_element_type=F32)
        b = jnp.dot(hh, wu_ref[:, c:c + FF_CHUNK], preferred_element_type=F32)
        return (a * jax.nn.sigmoid(a) * b).astype(BF16)

    mixes = [mix_ref[rows, :] + jnp.dot(ycat[rows, :], wo_ref[SB_WIDTH:, :],
                                        preferred_element_type=F32) for rows in _ROW_HALVES]
    x1s, hs, acts = [], [], []
    for rows, mix in zip(_ROW_HALVES, mixes):
        x1 = x_ref[0, rows, :] + _rms(mix, a_mix)
        hh = (_rms(x1, a_pre) + shift_f).astype(BF16)
        acts.append(swiglu(hh, 0))
        x1s.append(x1)
        hs.append(hh)
    h = jnp.concatenate(hs, axis=0)
    f = jnp.dot(jnp.concatenate(acts, axis=0), wd_ref[0:FF_CHUNK, :], preferred_element_type=F32)
    last = D_FF - FF_CHUNK
    for c in range(FF_CHUNK, last, FF_CHUNK):
        f += jnp.dot(swiglu(h, c), wd_ref[c:c + FF_CHUNK, :], preferred_element_type=F32)
    for rows, x1, hh in zip(_ROW_HALVES, x1s, hs):
        fh = f[rows, :] + jnp.dot(swiglu(hh, last), wd_ref[last:, :], preferred_element_type=F32)
        o_ref[0, rows, :] = x1 + _rms(fh, a_post)


def _tail(attn, u, x, mod3, w_pool_bf, pool_scale, w_out_bf, g_mix_post, g_ffn_pre, g_ffn_post,
          wg_bf, wu_bf, wd_bf):
    B, S, D = x.shape
    tm = TOKEN_TILE
    halo_blocks = tm // POOL_HALO
    resident = functools.partial(pl.BlockSpec, pipeline_mode=pl.Buffered(1))
    vec = pl.BlockSpec((1, D), lambda b, t: (0, 0))
    return pl.pallas_call(
        _tail_kernel,
        grid=(B, S // tm),
        in_specs=[
            pl.BlockSpec((1, tm, SB_WIDTH), lambda b, t: (b, t, 0)),
            pl.BlockSpec((1, tm, POOL_WIDTH), lambda b, t: (b, t, 0)),
            pl.BlockSpec((1, POOL_HALO, POOL_WIDTH),
                         lambda b, t: (b, jnp.maximum(t * halo_blocks - 1, 0), 0)),
            pl.BlockSpec((1, tm, D), lambda b, t: (b, t, 0)),
            pl.BlockSpec((1, 1, N_MOD * D), lambda b, t: (b, 0, 0)),
            resident(w_pool_bf.shape, lambda b, t: (0, 0, 0)),
            pl.BlockSpec((1, POOL_WIDTH), lambda b, t: (0, 0)),
            resident(w_out_bf.shape, lambda b, t: (0, 0)),
            vec, vec, vec,
            resident(wg_bf.shape, lambda b, t: (0, 0)),
            resident(wu_bf.shape, lambda b, t: (0, 0)),
            resident(wd_bf.shape, lambda b, t: (0, 0)),
        ],
        out_specs=pl.BlockSpec((1, tm, D), lambda b, t: (b, t, 0)),
        out_shape=jax.ShapeDtypeStruct((B, S, D), F32),
        scratch_shapes=[pltpu.VMEM((tm, D), F32)],
        compiler_params=pltpu.CompilerParams(
            dimension_semantics=("arbitrary", "arbitrary"), vmem_limit_bytes=VMEM_LIMIT),
        name="tail",
    )(attn, u, u, x, mod3, w_pool_bf, pool_scale, w_out_bf, g_mix_post, g_ffn_pre, g_ffn_post,
      wg_bf, wu_bf, wd_bf)


def kernel(x, c, w_cond, b_cond, g_mix_pre, g_mix_post, w_in, w_pool, pool_scale, w_out,
           g_ffn_pre, g_ffn_post, w_gate, w_up, w_down):
    B, S, D = x.shape
    depth = w_cond.shape[0]
    for l in range(depth):
        mod3 = _cond(c, w_cond[l], b_cond[l]).reshape(B, 1, N_MOD * D)
        q, k, vt, u = _proj(x, mod3, g_mix_pre[l].reshape(1, D), w_in[l].astype(BF16))
        attn = _attention(q, k, vt)
        x = _tail(attn, u, x, mod3, w_pool[l].astype(BF16), pool_scale[l].reshape(1, POOL_WIDTH),
                  w_out[l].astype(BF16), g_mix_post[l].reshape(1, D), g_ffn_pre[l].reshape(1, D),
                  g_ffn_post[l].reshape(1, D), w_gate[l].astype(BF16), w_up[l].astype(BF16),
                  w_down[l].astype(BF16))
    return x
```

```python
import functools
import math

import jax
import jax.numpy as jnp
import numpy as np
from jax import lax
from jax.experimental import pallas as pl
from jax.experimental.pallas import tpu as pltpu

D_MODEL = 1024
SB_WIDTH = 512
SB_HEADS = 8
SB_HEAD_DIM = 64
POOL_WIDTH = 512
POOL_WINDOWS = (2, 4, 8, 16)
POOL_GROUP_DIM = 128
D_FF = 2816
N_MOD = 6
EPS = 1e-6

LANES = 128
Q_TILE = 256
K_TILE = 256
LOG2E = math.log2(math.e)
SP_CLAMP = 100.0
DEAD_LOG2 = -151.0
HEADS_PER_GROUP = LANES // SB_HEAD_DIM
N_GROUPS = SB_HEADS // HEADS_PER_GROUP
TOKEN_TILE = 512
POOL_HALO = 16
FF_CHUNK = 256
VMEM_LIMIT = 56 * 1024 * 1024
_ROW_HALVES = (slice(0, TOKEN_TILE // 2), slice(TOKEN_TILE // 2, TOKEN_TILE))

F32 = jnp.float32
BF16 = jnp.bfloat16


def _split_bf16(a):
    hi = a.astype(BF16)
    lo = (a - hi.astype(F32)).astype(BF16)
    return hi, lo


def _rms(x, a):
    return x * lax.rsqrt(jnp.mean(x * x, axis=-1, keepdims=True) + EPS) * a


def _cond_kernel(c_ref, w_ref, b_ref, o_ref):
    c = c_ref[...]
    a = c * jax.nn.sigmoid(c)
    a_hi, a_lo = _split_bf16(a)
    w_hi, w_lo = _split_bf16(w_ref[...])
    acc = jnp.dot(a_hi, w_hi, preferred_element_type=F32)
    acc += jnp.dot(a_lo, w_hi, preferred_element_type=F32)
    acc += jnp.dot(a_hi, w_lo, preferred_element_type=F32)
    o_ref[...] = acc + b_ref[...]


def _cond(c, w_cond, b_cond):
    B, D = c.shape
    N = w_cond.shape[1]
    tn = 1024
    return pl.pallas_call(
        _cond_kernel,
        grid=(N // tn,),
        in_specs=[
            pl.BlockSpec((B, D), lambda n: (0, 0)),
            pl.BlockSpec((D, tn), lambda n: (0, n)),
            pl.BlockSpec((1, tn), lambda n: (0, n)),
        ],
        out_specs=pl.BlockSpec((B, tn), lambda n: (0, n)),
        out_shape=jax.ShapeDtypeStruct((B, N), F32),
        compiler_params=pltpu.CompilerParams(
            dimension_semantics=("arbitrary",), vmem_limit_bytes=VMEM_LIMIT),
        name="cond",
    )(c, w_cond, b_cond.reshape(1, N))


def _proj_kernel(x_ref, mod_ref, g_ref, w_ref, q_ref, k_ref, vt_ref, u_ref):
    shift = mod_ref[0, :, 0:D_MODEL]
    scale = mod_ref[0, :, D_MODEL:2 * D_MODEL]
    h = (_rms(x_ref[0], g_ref[...] * (1.0 + scale)) + shift).astype(BF16)
    q = jnp.dot(h, w_ref[:, 0:SB_WIDTH], preferred_element_type=F32)
    q_ref[0] = (q * (LOG2E / math.sqrt(SB_HEAD_DIM))).astype(BF16)
    k = jnp.dot(h, w_ref[:, SB_WIDTH:2 * SB_WIDTH], preferred_element_type=F32)
    k_ref[0] = k.astype(BF16)
    v = jnp.dot(h, w_ref[:, 2 * SB_WIDTH:3 * SB_WIDTH], preferred_element_type=F32)
    for t in range(TOKEN_TILE // K_TILE):
        vt_ref[0, t] = v[t * K_TILE:(t + 1) * K_TILE, :].T.astype(BF16)
    u_ref[0] = jnp.dot(h, w_ref[:, 3 * SB_WIDTH:], preferred_element_type=F32)


def _proj(x, mod3, g_pre, w_in_bf):
    B, S, D = x.shape
    tm = TOKEN_TILE
    n_kt = S // K_TILE
    return pl.pallas_call(
        _proj_kernel,
        grid=(B, S // tm),
        in_specs=[
            pl.BlockSpec((1, tm, D), lambda b, t: (b, t, 0)),
            pl.BlockSpec((1, 1, N_MOD * D), lambda b, t: (b, 0, 0)),
            pl.BlockSpec((1, D), lambda b, t: (0, 0)),
            pl.BlockSpec(w_in_bf.shape, lambda b, t: (0, 0)),
        ],
        out_specs=[
            pl.BlockSpec((1, tm, SB_WIDTH), lambda b, t: (b, t, 0)),
            pl.BlockSpec((1, tm, SB_WIDTH), lambda b, t: (b, t, 0)),
            pl.BlockSpec((1, tm // K_TILE, SB_WIDTH, K_TILE), lambda b, t: (b, t, 0, 0)),
            pl.BlockSpec((1, tm, POOL_WIDTH), lambda b, t: (b, t, 0)),
        ],
        out_shape=[
            jax.ShapeDtypeStruct((B, S, SB_WIDTH), BF16),
            jax.ShapeDtypeStruct((B, S, SB_WIDTH), BF16),
            jax.ShapeDtypeStruct((B, n_kt, SB_WIDTH, K_TILE), BF16),
            jax.ShapeDtypeStruct((B, S, POOL_WIDTH), F32),
        ],
        compiler_params=pltpu.CompilerParams(
            dimension_semantics=("arbitrary", "arbitrary"), vmem_limit_bytes=VMEM_LIMIT),
        name="proj",
    )(x, mod3, g_pre, w_in_bf)


def _attn_kernel(q_ref, k_ref, vt_ref, a_ref, o_ref, acc_ref, carry_ref, z_ref, x_ref, sc_ref):
    lane = lax.broadcasted_iota(jnp.int32, (Q_TILE, LANES), 1)
    zero = jnp.zeros((), BF16)
    a_mat = a_ref[...]
    diag_mask = (lax.broadcasted_iota(jnp.int32, (K_TILE, Q_TILE), 0)
                 < lax.broadcasted_iota(jnp.int32, (K_TILE, Q_TILE), 1))

    def query_tile(j_diag, _):
        q_rows = pl.ds(pl.multiple_of(j_diag * Q_TILE, Q_TILE), Q_TILE)
        q = q_ref[0, q_rows, :]
        qh = []
        for h in range(SB_HEADS):
            g, hh = divmod(h, HEADS_PER_GROUP)
            qg = q[:, g * LANES:(g + 1) * LANES]
            qh.append(jnp.where((lane // SB_HEAD_DIM) == hh, qg, zero))
        _walk_keys(j_diag, qh, k_ref, vt_ref, a_mat, diag_mask, acc_ref, carry_ref, z_ref, x_ref,
                   sc_ref)
        outs = []
        for g in range(N_GROUPS):
            sel = jnp.concatenate(
                [acc_ref[g * HEADS_PER_GROUP + hh] for hh in range(HEADS_PER_GROUP)],
                axis=0)
            outs.append(sel.T)
        o_ref[0, q_rows, :] = jnp.concatenate(outs, axis=1).astype(o_ref.dtype)
        return 0

    lax.fori_loop(0, q_ref.shape[1] // Q_TILE, query_tile, 0)


def _walk_keys(j_diag, qh, k_ref, vt_ref, a_mat, diag_mask, acc_ref, carry_ref, z_ref, x_ref,
               sc_ref):
    def logits(j, h):
        row0 = pl.multiple_of(j * K_TILE, K_TILE)
        g = h // HEADS_PER_GROUP
        kj = k_ref[0, pl.ds(row0, K_TILE), g * LANES:(g + 1) * LANES]
        return lax.dot_general(kj, qh[h], (((1,), (1,)), ((), ())), preferred_element_type=F32)

    def softplus2(z):
        return jnp.maximum(z, jnp.log(1.0 + jnp.exp2(jnp.minimum(z, SP_CLAMP))) * LOG2E)

    def values(j, h):
        return vt_ref[0, j, h * SB_HEAD_DIM:(h + 1) * SB_HEAD_DIM, :]

    def step(j, diagonal):
        j_next = jnp.maximum(j - 1, 0)

        for s in range(SB_HEADS):
            z = z_ref[s]
            z_ref[s] = logits(j_next, s)
            sp = softplus2(z)
            if diagonal:
                sp = jnp.where(diag_mask, sp, 0.0)
            cs = jnp.dot(a_mat, sp.astype(BF16), preferred_element_type=F32)
            x = z + cs
            if diagonal:
                x_ref[s] = jnp.where(diag_mask, x, -jnp.inf)
                carry_ref[s] = cs[0:1, :]
            else:
                x_ref[s] = x
                carry = carry_ref[s]
                carry_ref[s] = carry + cs[0:1, :]
                sc_ref[s] = jnp.exp2(carry)
        live = (jnp.max(carry_ref[...]) > DEAD_LOG2).astype(jnp.int32)
        for s in range(SB_HEADS):
            w = jnp.exp2(x_ref[s]).astype(BF16)
            pv = jnp.dot(values(j, s), w, preferred_element_type=F32)
            if diagonal:
                acc_ref[s] = pv
            else:
                acc_ref[s] += pv * sc_ref[s]
        return live

    for h in range(SB_HEADS):
        z_ref[h] = logits(j_diag, h)
    live0 = step(j_diag, True)

    def cond(state):
        n, live = state
        return jnp.logical_and(n < j_diag, live > 0)

    def body(state):
        n, _ = state
        return n + 1, step(j_diag - 1 - n, False)

    lax.while_loop(cond, body, (jnp.int32(0), live0))


def _cumsum_matrix():
    r = np.arange(K_TILE)
    return jnp.asarray(np.where(r[None, :] >= r[:, None], -1.0, 0.0), dtype=BF16)


def _attention(q, k, vt):
    B, S, _ = q.shape
    n_kt = S // K_TILE
    return pl.pallas_call(
        _attn_kernel,
        grid=(B,),
        in_specs=[
            pl.BlockSpec((1, S, SB_WIDTH), lambda b: (b, 0, 0)),
            pl.BlockSpec((1, S, SB_WIDTH), lambda b: (b, 0, 0)),
            pl.BlockSpec((1, n_kt, SB_WIDTH, K_TILE), lambda b: (b, 0, 0, 0)),
            pl.BlockSpec((K_TILE, K_TILE), lambda b: (0, 0)),
        ],
        out_specs=pl.BlockSpec((1, S, SB_WIDTH), lambda b: (b, 0, 0)),
        out_shape=jax.ShapeDtypeStruct((B, S, SB_WIDTH), BF16),
        scratch_shapes=[
            pltpu.VMEM((SB_HEADS, SB_HEAD_DIM, Q_TILE), F32),
            pltpu.VMEM((SB_HEADS, 1, Q_TILE), F32),
            pltpu.VMEM((SB_HEADS, K_TILE, Q_TILE), F32),
            pltpu.VMEM((SB_HEADS, K_TILE, Q_TILE), F32),
            pltpu.VMEM((SB_HEADS, 1, Q_TILE), F32),
        ],
        compiler_params=pltpu.CompilerParams(
            dimension_semantics=("arbitrary",), vmem_limit_bytes=VMEM_LIMIT),
        name="attn",
    )(q, k, vt, _cumsum_matrix())


def _tail_kernel(attn_ref, u_ref, uh_ref, x_ref, mod_ref, wp_ref, ps_ref, wo_ref, gmix_ref,
                 gpre_ref, gpost_ref, wg_ref, wu_ref, wd_ref, o_ref, mix_ref):
    t = pl.program_id(1)
    tm = TOKEN_TILE
    u = u_ref[0]
    halo = jnp.where(t > 0, uh_ref[0], 0.0)
    row = t * tm + lax.broadcasted_iota(jnp.int32, (tm, 1), 0)
    ys = []
    n_chunk = D_MODEL // len(POOL_WINDOWS)
    for g, win in enumerate(POOL_WINDOWS):
        ocols = slice(g * n_chunk, (g + 1) * n_chunk)
        mix_ref[:, ocols] = jnp.dot(attn_ref[0], wo_ref[0:SB_WIDTH, ocols],
                                    preferred_element_type=F32)
        cols = slice(g * POOL_GROUP_DIM, (g + 1) * POOL_GROUP_DIM)
        s = jnp.concatenate([halo[:, cols], u[:, cols]], axis=0)
        w = 1
        while w < win:
            s = s + pltpu.roll(s, w, axis=0)
            w *= 2
        inv_cnt = 1.0 / jnp.minimum(row + 1, win).astype(F32)
        pooled = s[POOL_HALO:, :] * inv_cnt - u[:, cols]
        y = jnp.dot(pooled.astype(BF16), wp_ref[g], preferred_element_type=F32) * ps_ref[:, cols]
        ys.append(y.astype(BF16))
    ycat = jnp.concatenate(ys, axis=1)

    shift_f = mod_ref[0, :, 3 * D_MODEL:4 * D_MODEL]
    a_mix = gmix_ref[...] * mod_ref[0, :, 2 * D_MODEL:3 * D_MODEL]
    a_pre = gpre_ref[...] * (1.0 + mod_ref[0, :, 4 * D_MODEL:5 * D_MODEL])
    a_post = gpost_ref[...] * mod_ref[0, :, 5 * D_MODEL:6 * D_MODEL]

    def swiglu(hh, c):
        a = jnp.dot(hh, wg_ref[:, c:c + FF_CHUNK], preferred_element_type=F32)
        b = jnp.dot(hh, wu_ref[:, c:c + FF_CHUNK], preferred_element_type=F32)
        return (a * jax.nn.sigmoid(a) * b).astype(BF16)

    def down(act, c):
        return jnp.dot(act, wd_ref[c:c + FF_CHUNK, :], preferred_element_type=F32)

    mixes = [mix_ref[rows, :] + jnp.dot(ycat[rows, :], wo_ref[SB_WIDTH:, :],
                                        preferred_element_type=F32) for rows in _ROW_HALVES]
    x1s, hs, acts = [], [], []
    for rows, mix in zip(_ROW_HALVES, mixes):
        x1 = x_ref[0, rows, :] + _rms(mix, a_mix)
        hh = (_rms(x1, a_pre) + shift_f).astype(BF16)
        acts.append(swiglu(hh, 0))
        x1s.append(x1)
        hs.append(hh)
    h = jnp.concatenate(hs, axis=0)
    f = down(jnp.concatenate(acts, axis=0), 0)
    last = D_FF - FF_CHUNK
    for c in range(FF_CHUNK, last, FF_CHUNK):
        f += down(swiglu(h, c), c)
    for rows, x1, hh in zip(_ROW_HALVES, x1s, hs):
        fh = f[rows, :] + down(swiglu(hh, last), last)
        o_ref[0, rows, :] = x1 + _rms(fh, a_post)


def _tail(attn, u, x, mod3, w_pool_bf, pool_scale, w_out_bf, g_mix_post, g_ffn_pre, g_ffn_post,
          wg_bf, wu_bf, wd_bf):
    B, S, D = x.shape
    tm = TOKEN_TILE
    halo_blocks = tm // POOL_HALO
    resident = functools.partial(pl.BlockSpec, pipeline_mode=pl.Buffered(1))
    vec = pl.BlockSpec((1, D), lambda b, t: (0, 0))
    return pl.pallas_call(
        _tail_kernel,
        grid=(B, S // tm),
        in_specs=[
            pl.BlockSpec((1, tm, SB_WIDTH), lambda b, t: (b, t, 0)),
            pl.BlockSpec((1, tm, POOL_WIDTH), lambda b, t: (b, t, 0)),
            pl.BlockSpec((1, POOL_HALO, POOL_WIDTH),
                         lambda b, t: (b, jnp.maximum(t * halo_blocks - 1, 0), 0)),
            pl.BlockSpec((1, tm, D), lambda b, t: (b, t, 0)),
            pl.BlockSpec((1, 1, N_MOD * D), lambda b, t: (b, 0, 0)),
            resident(w_pool_bf.shape, lambda b, t: (0, 0, 0)),
            pl.BlockSpec((1, POOL_WIDTH), lambda b, t: (0, 0)),
            resident(w_out_bf.shape, lambda b, t: (0, 0)),
            vec, vec, vec,
            resident(wg_bf.shape, lambda b, t: (0, 0)),
            resident(wu_bf.shape, lambda b, t: (0, 0)),
            resident(wd_bf.shape, lambda b, t: (0, 0)),
        ],
        out_specs=pl.BlockSpec((1, tm, D), lambda b, t: (b, t, 0)),
        out_shape=jax.ShapeDtypeStruct((B, S, D), F32),
        scratch_shapes=[pltpu.VMEM((tm, D), F32)],
        compiler_params=pltpu.CompilerParams(
            dimension_semantics=("arbitrary", "arbitrary"), vmem_limit_bytes=VMEM_LIMIT),
        name="tail",
    )(attn, u, u, x, mod3, w_pool_bf, pool_scale, w_out_bf, g_mix_post, g_ffn_pre, g_ffn_post,
      wg_bf, wu_bf, wd_bf)


def kernel(x, c, w_cond, b_cond, g_mix_pre, g_mix_post, w_in, w_pool, pool_scale, w_out,
           g_ffn_pre, g_ffn_post, w_gate, w_up, w_down):
    B, S, D = x.shape
    depth = w_cond.shape[0]
    for l in range(depth):
        mod3 = _cond(c, w_cond[l], b_cond[l]).reshape(B, 1, N_MOD * D)
        q, k, vt, u = _proj(x, mod3, g_mix_pre[l].reshape(1, D), w_in[l].astype(BF16))
        attn = _attention(q, k, vt)
        x = _tail(attn, u, x, mod3, w_pool[l].astype(BF16), pool_scale[l].reshape(1, POOL_WIDTH),
                  w_out[l].astype(BF16), g_mix_post[l].reshape(1, D), g_ffn_pre[l].reshape(1, D),
                  g_ffn_post[l].reshape(1, D), w_gate[l].astype(BF16), w_up[l].astype(BF16),
                  w_down[l].astype(BF16))
    return x
```

```python
import functools
import math

import jax
import jax.numpy as jnp
import numpy as np
from jax import lax
from jax.experimental import pallas as pl
from jax.experimental.pallas import tpu as pltpu

D_MODEL = 1024
SB_WIDTH = 512
SB_HEADS = 8
SB_HEAD_DIM = 64
POOL_WIDTH = 512
POOL_WINDOWS = (2, 4, 8, 16)
POOL_GROUP_DIM = 128
D_FF = 2816
N_MOD = 6
EPS = 1e-6

LANES = 128
Q_TILE = 256
K_TILE = 256
LOG2E = math.log2(math.e)
SP_CLAMP = 100.0
DEAD_LOG2 = -151.0
HEADS_PER_GROUP = LANES // SB_HEAD_DIM
N_GROUPS = SB_HEADS // HEADS_PER_GROUP
TOKEN_TILE = 512
POOL_HALO = 16
FF_CHUNK = 256
VMEM_LIMIT = 56 * 1024 * 1024
_ROW_HALVES = (slice(0, TOKEN_TILE // 2), slice(TOKEN_TILE // 2, TOKEN_TILE))

F32 = jnp.float32
BF16 = jnp.bfloat16


def _split_bf16(a):
    hi = a.astype(BF16)
    lo = (a - hi.astype(F32)).astype(BF16)
    return hi, lo


def _rms(x, a):
    return x * lax.rsqrt(jnp.mean(x * x, axis=-1, keepdims=True) + EPS) * a


def _cond_kernel(c_ref, w_ref, b_ref, o_ref):
    c = c_ref[...]
    a = c * jax.nn.sigmoid(c)
    a_hi, a_lo = _split_bf16(a)
    w_hi, w_lo = _split_bf16(w_ref[...])
    acc = jnp.dot(a_hi, w_hi, preferred_element_type=F32)
    acc += jnp.dot(a_lo, w_hi, preferred_element_type=F32)
    acc += jnp.dot(a_hi, w_lo, preferred_element_type=F32)
    o_ref[...] = acc + b_ref[...]


def _cond(c, w_cond, b_cond):
    B, D = c.shape
    N = w_cond.shape[1]
    tn = 1024
    return pl.pallas_call(
        _cond_kernel,
        grid=(N // tn,),
        in_specs=[
            pl.BlockSpec((B, D), lambda n: (0, 0)),
            pl.BlockSpec((D, tn), lambda n: (0, n)),
            pl.BlockSpec((1, tn), lambda n: (0, n)),
        ],
        out_specs=pl.BlockSpec((B, tn), lambda n: (0, n)),
        out_shape=jax.ShapeDtypeStruct((B, N), F32),
        compiler_params=pltpu.CompilerParams(
            dimension_semantics=("arbitrary",), vmem_limit_bytes=VMEM_LIMIT),
        name="cond",
    )(c, w_cond, b_cond.reshape(1, N))


def _proj_kernel(x_ref, mod_ref, g_ref, w_ref, q_ref, k_ref, vt_ref, u_ref):
    shift = mod_ref[0, :, 0:D_MODEL]
    scale = mod_ref[0, :, D_MODEL:2 * D_MODEL]
    h = (_rms(x_ref[0], g_ref[...] * (1.0 + scale)) + shift).astype(BF16)
    q = jnp.dot(h, w_ref[:, 0:SB_WIDTH], preferred_element_type=F32)
    q_ref[0] = (q * (LOG2E / math.sqrt(SB_HEAD_DIM))).astype(BF16)
    k = jnp.dot(h, w_ref[:, SB_WIDTH:2 * SB_WIDTH], preferred_element_type=F32)
    k_ref[0] = k.astype(BF16)
    v = jnp.dot(h, w_ref[:, 2 * SB_WIDTH:3 * SB_WIDTH], preferred_element_type=F32)
    for t in range(TOKEN_TILE // K_TILE):
        vt_ref[0, t] = v[t * K_TILE:(t + 1) * K_TILE, :].T.astype(BF16)
    u_ref[0] = jnp.dot(h, w_ref[:, 3 * SB_WIDTH:], preferred_element_type=F32)


def _proj(x, mod3, g_pre, w_in_bf):
    B, S, D = x.shape
    tm = TOKEN_TILE
    n_kt = S // K_TILE
    return pl.pallas_call(
        _proj_kernel,
        grid=(B, S // tm),
        in_specs=[
            pl.BlockSpec((1, tm, D), lambda b, t: (b, t, 0)),
            pl.BlockSpec((1, 1, N_MOD * D), lambda b, t: (b, 0, 0)),
            pl.BlockSpec((1, D), lambda b, t: (0, 0)),
            pl.BlockSpec(w_in_bf.shape, lambda b, t: (0, 0)),
        ],
        out_specs=[
            pl.BlockSpec((1, tm, SB_WIDTH), lambda b, t: (b, t, 0)),
            pl.BlockSpec((1, tm, SB_WIDTH), lambda b, t: (b, t, 0)),
            pl.BlockSpec((1, tm // K_TILE, SB_WIDTH, K_TILE), lambda b, t: (b, t, 0, 0)),
            pl.BlockSpec((1, tm, POOL_WIDTH), lambda b, t: (b, t, 0)),
        ],
        out_shape=[
            jax.ShapeDtypeStruct((B, S, SB_WIDTH), BF16),
            jax.ShapeDtypeStruct((B, S, SB_WIDTH), BF16),
            jax.ShapeDtypeStruct((B, n_kt, SB_WIDTH, K_TILE), BF16),
            jax.ShapeDtypeStruct((B, S, POOL_WIDTH), F32),
        ],
        compiler_params=pltpu.CompilerParams(
            dimension_semantics=("arbitrary", "arbitrary"), vmem_limit_bytes=VMEM_LIMIT),
        name="proj",
    )(x, mod3, g_pre, w_in_bf)


def _attn_kernel(q_ref, k_ref, vt_ref, a_ref, o_ref, acc_ref, carry_ref, z_ref, x_ref, sc_ref):
    lane = lax.broadcasted_iota(jnp.int32, (Q_TILE, LANES), 1)
    zero = jnp.zeros((), BF16)
    a_mat = a_ref[...]
    diag_mask = (lax.broadcasted_iota(jnp.int32, (K_TILE, Q_TILE), 0)
                 < lax.broadcasted_iota(jnp.int32, (K_TILE, Q_TILE), 1))

    def query_tile(j_diag, _):
        q_rows = pl.ds(pl.multiple_of(j_diag * Q_TILE, Q_TILE), Q_TILE)
        q = q_ref[0, q_rows, :]
        qh = []
        for h in range(SB_HEADS):
            g, hh = divmod(h, HEADS_PER_GROUP)
            qg = q[:, g * LANES:(g + 1) * LANES]
            qh.append(jnp.where((lane // SB_HEAD_DIM) == hh, qg, zero))
        _walk_keys(j_diag, qh, k_ref, vt_ref, a_mat, diag_mask, acc_ref, carry_ref, z_ref, x_ref,
                   sc_ref)
        outs = []
        for g in range(N_GROUPS):
            sel = jnp.concatenate(
                [acc_ref[g * HEADS_PER_GROUP + hh] for hh in range(HEADS_PER_GROUP)],
                axis=0)
            outs.append(sel.T)
        o_ref[0, q_rows, :] = jnp.concatenate(outs, axis=1).astype(o_ref.dtype)
        return 0

    lax.fori_loop(0, q_ref.shape[1] // Q_TILE, query_tile, 0)


def _walk_keys(j_diag, qh, k_ref, vt_ref, a_mat, diag_mask, acc_ref, carry_ref, z_ref, x_ref,
               sc_ref):
    def logits(j, h):
        row0 = pl.multiple_of(j * K_TILE, K_TILE)
        g = h // HEADS_PER_GROUP
        kj = k_ref[0, pl.ds(row0, K_TILE), g * LANES:(g + 1) * LANES]
        return lax.dot_general(kj, qh[h], (((1,), (1,)), ((), ())), preferred_element_type=F32)

    def softplus2(z):
        return jnp.maximum(z, jnp.log(1.0 + jnp.exp2(jnp.minimum(z, SP_CLAMP))) * LOG2E)

    def values(j, h):
        return vt_ref[0, j, h * SB_HEAD_DIM:(h + 1) * SB_HEAD_DIM, :]

    def step(j, diagonal):
        j_next = jnp.maximum(j - 1, 0)

        for s in range(SB_HEADS):
            z = z_ref[s]
            z_ref[s] = logits(j_next, s)
            sp = softplus2(z)
            if diagonal:
                sp = jnp.where(diag_mask, sp, 0.0)
            x_ref[s] = z - sp
            sp0 = sp[0:1, :]
            cs = jnp.dot(a_mat, sp.astype(BF16), preferred_element_type=F32)
            x = x_ref[s] + cs
            tot = cs[0:1, :] - sp0
            if diagonal:
                x_ref[s] = jnp.where(diag_mask, x, -jnp.inf)
                carry_ref[s] = tot
            else:
                x_ref[s] = x
                carry = carry_ref[s]
                carry_ref[s] = carry + tot
                sc_ref[s] = jnp.exp2(carry)
        live = (jnp.max(carry_ref[...]) > DEAD_LOG2).astype(jnp.int32)
        for s in range(SB_HEADS):
            w = jnp.exp2(x_ref[s]).astype(BF16)
            pv = jnp.dot(values(j, s), w, preferred_element_type=F32)
            if diagonal:
                acc_ref[s] = pv
            else:
                acc_ref[s] += pv * sc_ref[s]
        return live

    for h in range(SB_HEADS):
        z_ref[h] = logits(j_diag, h)
    live0 = step(j_diag, True)

    def cond(state):
        n, live = state
        return jnp.logical_and(n < j_diag, live > 0)

    def body(state):
        n, _ = state
        return n + 1, step(j_diag - 1 - n, False)

    lax.while_loop(cond, body, (jnp.int32(0), live0))


def _cumsum_matrix():
    r = np.arange(K_TILE)
    return jnp.asarray(np.where(r[None, :] > r[:, None], -1.0, 0.0), dtype=BF16)


def _attention(q, k, vt):
    B, S, _ = q.shape
    n_kt = S // K_TILE
    return pl.pallas_call(
        _attn_kernel,
        grid=(B,),
        in_specs=[
            pl.BlockSpec((1, S, SB_WIDTH), lambda b: (b, 0, 0)),
            pl.BlockSpec((1, S, SB_WIDTH), lambda b: (b, 0, 0)),
            pl.BlockSpec((1, n_kt, SB_WIDTH, K_TILE), lambda b: (b, 0, 0, 0)),
            pl.BlockSpec((K_TILE, K_TILE), lambda b: (0, 0)),
        ],
        out_specs=pl.BlockSpec((1, S, SB_WIDTH), lambda b: (b, 0, 0)),
        out_shape=jax.ShapeDtypeStruct((B, S, SB_WIDTH), BF16),
        scratch_shapes=[
            pltpu.VMEM((SB_HEADS, SB_HEAD_DIM, Q_TILE), F32),
            pltpu.VMEM((SB_HEADS, 1, Q_TILE), F32),
            pltpu.VMEM((SB_HEADS, K_TILE, Q_TILE), F32),
            pltpu.VMEM((SB_HEADS, K_TILE, Q_TILE), F32),
            pltpu.VMEM((SB_HEADS, 1, Q_TILE), F32),
        ],
        compiler_params=pltpu.CompilerParams(
            dimension_semantics=("arbitrary",), vmem_limit_bytes=VMEM_LIMIT),
        name="attn",
    )(q, k, vt, _cumsum_matrix())


def _tail_kernel(attn_ref, u_ref, uh_ref, x_ref, mod_ref, wp_ref, ps_ref, wo_ref, gmix_ref,
                 gpre_ref, gpost_ref, wg_ref, wu_ref, wd_ref, o_ref, mix_ref):
    t = pl.program_id(1)
    tm = TOKEN_TILE
    u = u_ref[0]
    halo = jnp.where(t > 0, uh_ref[0], 0.0)
    row = t * tm + lax.broadcasted_iota(jnp.int32, (tm, 1), 0)
    ys = []
    n_chunk = D_MODEL // len(POOL_WINDOWS)
    for g, win in enumerate(POOL_WINDOWS):
        ocols = slice(g * n_chunk, (g + 1) * n_chunk)
        mix_ref[:, ocols] = jnp.dot(attn_ref[0], wo_ref[0:SB_WIDTH, ocols],
                                    preferred_element_type=F32)
        cols = slice(g * POOL_GROUP_DIM, (g + 1) * POOL_GROUP_DIM)
        s = jnp.concatenate([halo[:, cols], u[:, cols]], axis=0)
        w = 1
        while w < win:
            s = s + pltpu.roll(s, w, axis=0)
            w *= 2
        inv_cnt = 1.0 / jnp.minimum(row + 1, win).astype(F32)
        pooled = s[POOL_HALO:, :] * inv_cnt - u[:, cols]
        y = jnp.dot(pooled.astype(BF16), wp_ref[g], preferred_element_type=F32) * ps_ref[:, cols]
        ys.append(y.astype(BF16))
    ycat = jnp.concatenate(ys, axis=1)

    shift_f = mod_ref[0, :, 3 * D_MODEL:4 * D_MODEL]
    a_mix = gmix_ref[...] * mod_ref[0, :, 2 * D_MODEL:3 * D_MODEL]
    a_pre = gpre_ref[...] * (1.0 + mod_ref[0, :, 4 * D_MODEL:5 * D_MODEL])
    a_post = gpost_ref[...] * mod_ref[0, :, 5 * D_MODEL:6 * D_MODEL]

    def swiglu(hh, c):
        a = jnp.dot(hh, wg_ref[:, c:c + FF_CHUNK], preferred_element_type=F32)
        b = jnp.dot(hh, wu_ref[:, c:c + FF_CHUNK], preferred_element_type=F32)
        return (a * jax.nn.sigmoid(a) * b).astype(BF16)

    def down(act, c):
        return jnp.dot(act, wd_ref[c:c + FF_CHUNK, :], preferred_element_type=F32)

    mixes = [mix_ref[rows, :] + jnp.dot(ycat[rows, :], wo_ref[SB_WIDTH:, :],
                                        preferred_element_type=F32) for rows in _ROW_HALVES]
    x1s, hs, acts = [], [], []
    for rows, mix in zip(_ROW_HALVES, mixes):
        x1 = x_ref[0, rows, :] + _rms(mix, a_mix)
        hh = (_rms(x1, a_pre) + shift_f).astype(BF16)
        acts.append(swiglu(hh, 0))
        x1s.append(x1)
        hs.append(hh)
    h = jnp.concatenate(hs, axis=0)
    f = down(jnp.concatenate(acts, axis=0), 0)
    last = D_FF - FF_CHUNK
    for c in range(FF_CHUNK, last, FF_CHUNK):
        f += down(swiglu(h, c), c)
    for rows, x1, hh in zip(_ROW_HALVES, x1s, hs):
        fh = f[rows, :] + down(swiglu(hh, last), last)
        o_ref[0, rows, :] = x1 + _rms(fh, a_post)


def _tail(attn, u, x, mod3, w_pool_bf, pool_scale, w_out_bf, g_mix_post, g_ffn_pre, g_ffn_post,
          wg_bf, wu_bf, wd_bf):
    B, S, D = x.shape
    tm = TOKEN_TILE
    halo_blocks = tm // POOL_HALO
    resident = functools.partial(pl.BlockSpec, pipeline_mode=pl.Buffered(1))
    vec = pl.BlockSpec((1, D), lambda b, t: (0, 0))
    return pl.pallas_call(
        _tail_kernel,
        grid=(B, S // tm),
        in_specs=[
            pl.BlockSpec((1, tm, SB_WIDTH), lambda b, t: (b, t, 0)),
            pl.BlockSpec((1, tm, POOL_WIDTH), lambda b, t: (b, t, 0)),
            pl.BlockSpec((1, POOL_HALO, POOL_WIDTH),
                         lambda b, t: (b, jnp.maximum(t * halo_blocks - 1, 0), 0)),
            pl.BlockSpec((1, tm, D), lambda b, t: (b, t, 0)),
            pl.BlockSpec((1, 1, N_MOD * D), lambda b, t: (b, 0, 0)),
            resident(w_pool_bf.shape, lambda b, t: (0, 0, 0)),
            pl.BlockSpec((1, POOL_WIDTH), lambda b, t: (0, 0)),
            resident(w_out_bf.shape, lambda b, t: (0, 0)),
            vec, vec, vec,
            resident(wg_bf.shape, lambda b, t: (0, 0)),
            resident(wu_bf.shape, lambda b, t: (0, 0)),
            resident(wd_bf.shape, lambda b, t: (0, 0)),
        ],
        out_specs=pl.BlockSpec((1, tm, D), lambda b, t: (b, t, 0)),
        out_shape=jax.ShapeDtypeStruct((B, S, D), F32),
        scratch_shapes=[pltpu.VMEM((tm, D), F32)],
        compiler_params=pltpu.CompilerParams(
            dimension_semantics=("arbitrary", "arbitrary"), vmem_limit_bytes=VMEM_LIMIT),
        name="tail",
    )(attn, u, u, x, mod3, w_pool_bf, pool_scale, w_out_bf, g_mix_post, g_ffn_pre, g_ffn_post,
      wg_bf, wu_bf, wd_bf)


def kernel(x, c, w_cond, b_cond, g_mix_pre, g_mix_post, w_in, w_pool, pool_scale, w_out,
           g_ffn_pre, g_ffn_post, w_gate, w_up, w_down):
    B, S, D = x.shape
    depth = w_cond.shape[0]
    for l in range(depth):
        mod3 = _cond(c, w_cond[l], b_cond[l]).reshape(B, 1, N_MOD * D)
        q, k, vt, u = _proj(x, mod3, g_mix_pre[l].reshape(1, D), w_in[l].astype(BF16))
        attn = _attention(q, k, vt)
        x = _tail(attn, u, x, mod3, w_pool[l].astype(BF16), pool_scale[l].reshape(1, POOL_WIDTH),
                  w_out[l].astype(BF16), g_mix_post[l].reshape(1, D), g_ffn_pre[l].reshape(1, D),
                  g_ffn_post[l].reshape(1, D), w_gate[l].astype(BF16), w_up[l].astype(BF16),
                  w_down[l].astype(BF16))
    return x
```

```python
import functools
import math

import jax
import jax.numpy as jnp
import numpy as np
from jax import lax
from jax.experimental import pallas as pl
from jax.experimental.pallas import tpu as pltpu

D_MODEL = 1024
SB_WIDTH = 512
SB_HEADS = 8
SB_HEAD_DIM = 64
POOL_WIDTH = 512
POOL_WINDOWS = (2, 4, 8, 16)
POOL_GROUP_DIM = 128
D_FF = 2816
N_MOD = 6
EPS = 1e-6

LANES = 128
Q_TILE = 256
K_TILE = 256
LOG2E = math.log2(math.e)
SP_CLAMP = 100.0
DEAD_LOG2 = -151.0
HEADS_PER_GROUP = LANES // SB_HEAD_DIM
N_GROUPS = SB_HEADS // HEADS_PER_GROUP
TOKEN_TILE = 512
PROJ_TILE = 1024
POOL_HALO = 16
FF_CHUNK = 256
VMEM_LIMIT = 56 * 1024 * 1024
_ROW_HALVES = (slice(0, TOKEN_TILE // 2), slice(TOKEN_TILE // 2, TOKEN_TILE))

F32 = jnp.float32
BF16 = jnp.bfloat16


def _split_bf16(a):
    hi = a.astype(BF16)
    lo = (a - hi.astype(F32)).astype(BF16)
    return hi, lo


def _rms(x, a):
    return x * lax.rsqrt(jnp.mean(x * x, axis=-1, keepdims=True) + EPS) * a


def _cond_kernel(c_ref, w_ref, b_ref, o_ref):
    c = c_ref[...]
    a = c * jax.nn.sigmoid(c)
    a_hi, a_lo = _split_bf16(a)
    w_hi, w_lo = _split_bf16(w_ref[...])
    n = a.shape[0]
    both = jnp.dot(jnp.concatenate([a_hi, a_lo], axis=0), w_hi, preferred_element_type=F32)
    acc = both[0:n] + both[n:] + jnp.dot(a_hi, w_lo, preferred_element_type=F32)
    o_ref[...] = acc + b_ref[...]


def _cond(c, w_cond, b_cond):
    B, D = c.shape
    N = w_cond.shape[1]
    tn = 1024
    return pl.pallas_call(
        _cond_kernel,
        grid=(N // tn,),
        in_specs=[
            pl.BlockSpec((B, D), lambda n: (0, 0)),
            pl.BlockSpec((D, tn), lambda n: (0, n)),
            pl.BlockSpec((1, tn), lambda n: (0, n)),
        ],
        out_specs=pl.BlockSpec((B, tn), lambda n: (0, n)),
        out_shape=jax.ShapeDtypeStruct((B, N), F32),
        compiler_params=pltpu.CompilerParams(
            dimension_semantics=("arbitrary",), vmem_limit_bytes=VMEM_LIMIT),
        name="cond",
    )(c, w_cond, b_cond.reshape(1, N))


def _proj_kernel(x_ref, mod_ref, g_ref, w_ref, q_ref, k_ref, vt_ref, u_ref):
    shift = mod_ref[0, :, 0:D_MODEL]
    scale = mod_ref[0, :, D_MODEL:2 * D_MODEL]
    h = (_rms(x_ref[0], g_ref[...] * (1.0 + scale)) + shift).astype(BF16)
    q = jnp.dot(h, w_ref[:, 0:SB_WIDTH], preferred_element_type=F32)
    q_ref[0] = (q * (LOG2E / math.sqrt(SB_HEAD_DIM))).astype(BF16)
    k = jnp.dot(h, w_ref[:, SB_WIDTH:2 * SB_WIDTH], preferred_element_type=F32)
    k_ref[0] = k.astype(BF16)
    v = jnp.dot(h, w_ref[:, 2 * SB_WIDTH:3 * SB_WIDTH], preferred_element_type=F32)
    for t in range(PROJ_TILE // K_TILE):
        vt_ref[0, t] = v[t * K_TILE:(t + 1) * K_TILE, :].T.astype(BF16)
    u_ref[0] = jnp.dot(h, w_ref[:, 3 * SB_WIDTH:], preferred_element_type=F32)


def _proj(x, mod3, g_pre, w_in_bf):
    B, S, D = x.shape
    tm = PROJ_TILE
    n_kt = S // K_TILE
    return pl.pallas_call(
        _proj_kernel,
        grid=(B, S // tm),
        in_specs=[
            pl.BlockSpec((1, tm, D), lambda b, t: (b, t, 0)),
            pl.BlockSpec((1, 1, N_MOD * D), lambda b, t: (b, 0, 0)),
            pl.BlockSpec((1, D), lambda b, t: (0, 0)),
            pl.BlockSpec(w_in_bf.shape, lambda b, t: (0, 0)),
        ],
        out_specs=[
            pl.BlockSpec((1, tm, SB_WIDTH), lambda b, t: (b, t, 0)),
            pl.BlockSpec((1, tm, SB_WIDTH), lambda b, t: (b, t, 0)),
            pl.BlockSpec((1, tm // K_TILE, SB_WIDTH, K_TILE), lambda b, t: (b, t, 0, 0)),
            pl.BlockSpec((1, tm, POOL_WIDTH), lambda b, t: (b, t, 0)),
        ],
        out_shape=[
            jax.ShapeDtypeStruct((B, S, SB_WIDTH), BF16),
            jax.ShapeDtypeStruct((B, S, SB_WIDTH), BF16),
            jax.ShapeDtypeStruct((B, n_kt, SB_WIDTH, K_TILE), BF16),
            jax.ShapeDtypeStruct((B, S, POOL_WIDTH), F32),
        ],
        compiler_params=pltpu.CompilerParams(
            dimension_semantics=("arbitrary", "arbitrary"), vmem_limit_bytes=VMEM_LIMIT),
        name="proj",
    )(x, mod3, g_pre, w_in_bf)


def _attn_kernel(q_ref, k_ref, vt_ref, a_ref, o_ref, acc_ref, carry_ref, z_ref, x_ref, sc_ref):
    lane = lax.broadcasted_iota(jnp.int32, (Q_TILE, LANES), 1)
    zero = jnp.zeros((), BF16)
    a_mat = a_ref[...]
    diag_mask = (lax.broadcasted_iota(jnp.int32, (K_TILE, Q_TILE), 0)
                 < lax.broadcasted_iota(jnp.int32, (K_TILE, Q_TILE), 1))

    def query_tile(j_diag, _):
        q_rows = pl.ds(pl.multiple_of(j_diag * Q_TILE, Q_TILE), Q_TILE)
        q = q_ref[0, q_rows, :]
        qh = []
        for h in range(SB_HEADS):
            g, hh = divmod(h, HEADS_PER_GROUP)
            qg = q[:, g * LANES:(g + 1) * LANES]
            qh.append(jnp.where((lane // SB_HEAD_DIM) == hh, qg, zero))
        _walk_keys(j_diag, qh, k_ref, vt_ref, a_mat, diag_mask, acc_ref, carry_ref, z_ref, x_ref,
                   sc_ref)
        outs = []
        for g in range(N_GROUPS):
            sel = jnp.concatenate(
                [acc_ref[g * HEADS_PER_GROUP + hh] for hh in range(HEADS_PER_GROUP)],
                axis=0)
            outs.append(sel.T)
        o_ref[0, q_rows, :] = jnp.concatenate(outs, axis=1).astype(o_ref.dtype)
        return 0

    lax.fori_loop(0, q_ref.shape[1] // Q_TILE, query_tile, 0)


def _walk_keys(j_diag, qh, k_ref, vt_ref, a_mat, diag_mask, acc_ref, carry_ref, z_ref, x_ref,
               sc_ref):
    def logits(j, h):
        row0 = pl.multiple_of(j * K_TILE, K_TILE)
        g = h // HEADS_PER_GROUP
        kj = k_ref[0, pl.ds(row0, K_TILE), g * LANES:(g + 1) * LANES]
        return lax.dot_general(kj, qh[h], (((1,), (1,)), ((), ())), preferred_element_type=F32)

    def softplus2(z):
        return jnp.maximum(z, jnp.log(1.0 + jnp.exp2(jnp.minimum(z, SP_CLAMP))) * LOG2E)

    def values(j, h):
        return vt_ref[0, j, h * SB_HEAD_DIM:(h + 1) * SB_HEAD_DIM, :]

    def step(j, diagonal):
        j_next = jnp.maximum(j - 1, 0)

        for s in range(SB_HEADS):
            z = z_ref[s]
            z_ref[s] = logits(j_next, s)
            sp = softplus2(z)
            if diagonal:
                sp = jnp.where(diag_mask, sp, 0.0)
            x_ref[s] = z - sp
            sp0 = sp[0:1, :]
            cs = jnp.dot(a_mat, sp.astype(BF16), preferred_element_type=F32)
            x = x_ref[s] + cs
            tot = cs[0:1, :] - sp0
            if diagonal:
                x_ref[s] = jnp.where(diag_mask, x, -jnp.inf)
                carry_ref[s] = tot
            else:
                x_ref[s] = x
                carry = carry_ref[s]
                carry_ref[s] = carry + tot
                sc_ref[s] = jnp.exp2(carry)
        live = (jnp.max(carry_ref[...]) > DEAD_LOG2).astype(jnp.int32)
        for s in range(SB_HEADS):
            w = jnp.exp2(x_ref[s]).astype(BF16)
            pv = jnp.dot(values(j, s), w, preferred_element_type=F32)
            if diagonal:
                acc_ref[s] = pv
            else:
                acc_ref[s] += pv * sc_ref[s]
        return live

    for h in range(SB_HEADS):
        z_ref[h] = logits(j_diag, h)
    live0 = step(j_diag, True)

    def cond(state):
        n, live = state
        return jnp.logical_and(n < j_diag, live > 0)

    def body(state):
        n, _ = state
        return n + 1, step(j_diag - 1 - n, False)

    lax.while_loop(cond, body, (jnp.int32(0), live0))


def _cumsum_matrix():
    r = np.arange(K_TILE)
    return jnp.asarray(np.where(r[None, :] > r[:, None], -1.0, 0.0), dtype=BF16)


def _attention(q, k, vt):
    B, S, _ = q.shape
    n_kt = S // K_TILE
    return pl.pallas_call(
        _attn_kernel,
        grid=(B,),
        in_specs=[
            pl.BlockSpec((1, S, SB_WIDTH), lambda b: (b, 0, 0)),
            pl.BlockSpec((1, S, SB_WIDTH), lambda b: (b, 0, 0)),
            pl.BlockSpec((1, n_kt, SB_WIDTH, K_TILE), lambda b: (b, 0, 0, 0)),
            pl.BlockSpec((K_TILE, K_TILE), lambda b: (0, 0)),
        ],
        out_specs=pl.BlockSpec((1, S, SB_WIDTH), lambda b: (b, 0, 0)),
        out_shape=jax.ShapeDtypeStruct((B, S, SB_WIDTH), BF16),
        scratch_shapes=[
            pltpu.VMEM((SB_HEADS, SB_HEAD_DIM, Q_TILE), F32),
            pltpu.VMEM((SB_HEADS, 1, Q_TILE), F32),
            pltpu.VMEM((SB_HEADS, K_TILE, Q_TILE), F32),
            pltpu.VMEM((SB_HEADS, K_TILE, Q_TILE), F32),
            pltpu.VMEM((SB_HEADS, 1, Q_TILE), F32),
        ],
        compiler_params=pltpu.CompilerParams(
            dimension_semantics=("arbitrary",), vmem_limit_bytes=VMEM_LIMIT),
        name="attn",
    )(q, k, vt, _cumsum_matrix())


def _tail_kernel(attn_ref, u_ref, uh_ref, x_ref, mod_ref, wp_ref, ps_ref, wo_ref, gmix_ref,
                 gpre_ref, gpost_ref, wg_ref, wu_ref, wd_ref, o_ref, mix_ref):
    t = pl.program_id(1)
    tm = TOKEN_TILE
    u = u_ref[0]
    halo = jnp.where(t > 0, uh_ref[0], 0.0)
    row = t * tm + lax.broadcasted_iota(jnp.int32, (tm, 1), 0)
    ys = []
    n_chunk = D_MODEL // len(POOL_WINDOWS)
    for g, win in enumerate(POOL_WINDOWS):
        ocols = slice(g * n_chunk, (g + 1) * n_chunk)
        mix_ref[:, ocols] = jnp.dot(attn_ref[0], wo_ref[0:SB_WIDTH, ocols],
                                    preferred_element_type=F32)
        cols = slice(g * POOL_GROUP_DIM, (g + 1) * POOL_GROUP_DIM)
        s = jnp.concatenate([halo[:, cols], u[:, cols]], axis=0)
        w = 1
        while w < win:
            s = s + pltpu.roll(s, w, axis=0)
            w *= 2
        inv_cnt = 1.0 / jnp.minimum(row + 1, win).astype(F32)
        pooled = s[POOL_HALO:, :] * inv_cnt - u[:, cols]
        y = jnp.dot(pooled.astype(BF16), wp_ref[g], preferred_element_type=F32) * ps_ref[:, cols]
        ys.append(y.astype(BF16))
    ycat = jnp.concatenate(ys, axis=1)

    shift_f = mod_ref[0, :, 3 * D_MODEL:4 * D_MODEL]
    a_mix = gmix_ref[...] * mod_ref[0, :, 2 * D_MODEL:3 * D_MODEL]
    a_pre = gpre_ref[...] * (1.0 + mod_ref[0, :, 4 * D_MODEL:5 * D_MODEL])
    a_post = gpost_ref[...] * mod_ref[0, :, 5 * D_MODEL:6 * D_MODEL]

    def swiglu(hh, c):
        a = jnp.dot(hh, wg_ref[:, c:c + FF_CHUNK], preferred_element_type=F32)
        b = jnp.dot(hh, wu_ref[:, c:c + FF_CHUNK], preferred_element_type=F32)
        return (a * jax.nn.sigmoid(a) * b).astype(BF16)

    def down(act, c):
        return jnp.dot(act, wd_ref[c:c + FF_CHUNK, :], preferred_element_type=F32)

    mixes = [mix_ref[rows, :] + jnp.dot(ycat[rows, :], wo_ref[SB_WIDTH:, :],
                                        preferred_element_type=F32) for rows in _ROW_HALVES]
    x1s, hs, acts = [], [], []
    for rows, mix in zip(_ROW_HALVES, mixes):
        x1 = x_ref[0, rows, :] + _rms(mix, a_mix)
        hh = (_rms(x1, a_pre) + shift_f).astype(BF16)
        acts.append(swiglu(hh, 0))
        x1s.append(x1)
        hs.append(hh)
    h = jnp.concatenate(hs, axis=0)
    f = down(jnp.concatenate(acts, axis=0), 0)
    last = D_FF - FF_CHUNK
    for c in range(FF_CHUNK, last, FF_CHUNK):
        f += down(swiglu(h, c), c)
    for rows, x1, hh in zip(_ROW_HALVES, x1s, hs):
        fh = f[rows, :] + down(swiglu(hh, last), last)
        o_ref[0, rows, :] = x1 + _rms(fh, a_post)


def _tail(attn, u, x, mod3, w_pool_bf, pool_scale, w_out_bf, g_mix_post, g_ffn_pre, g_ffn_post,
          wg_bf, wu_bf, wd_bf):
    B, S, D = x.shape
    tm = TOKEN_TILE
    halo_blocks = tm // POOL_HALO
    resident = functools.partial(pl.BlockSpec, pipeline_mode=pl.Buffered(1))
    vec = pl.BlockSpec((1, D), lambda b, t: (0, 0))
    return pl.pallas_call(
        _tail_kernel,
        grid=(B, S // tm),
        in_specs=[
            pl.BlockSpec((1, tm, SB_WIDTH), lambda b, t: (b, t, 0)),
            pl.BlockSpec((1, tm, POOL_WIDTH), lambda b, t: (b, t, 0)),
            pl.BlockSpec((1, POOL_HALO, POOL_WIDTH),
                         lambda b, t: (b, jnp.maximum(t * halo_blocks - 1, 0), 0)),
            pl.BlockSpec((1, tm, D), lambda b, t: (b, t, 0)),
            pl.BlockSpec((1, 1, N_MOD * D), lambda b, t: (b, 0, 0)),
            resident(w_pool_bf.shape, lambda b, t: (0, 0, 0)),
            pl.BlockSpec((1, POOL_WIDTH), lambda b, t: (0, 0)),
            resident(w_out_bf.shape, lambda b, t: (0, 0)),
            vec, vec, vec,
            resident(wg_bf.shape, lambda b, t: (0, 0)),
            resident(wu_bf.shape, lambda b, t: (0, 0)),
            resident(wd_bf.shape, lambda b, t: (0, 0)),
        ],
        out_specs=pl.BlockSpec((1, tm, D), lambda b, t: (b, t, 0)),
        out_shape=jax.ShapeDtypeStruct((B, S, D), F32),
        scratch_shapes=[pltpu.VMEM((tm, D), F32)],
        compiler_params=pltpu.CompilerParams(
            dimension_semantics=("arbitrary", "arbitrary"), vmem_limit_bytes=VMEM_LIMIT),
        name="tail",
    )(attn, u, u, x, mod3, w_pool_bf, pool_scale, w_out_bf, g_mix_post, g_ffn_pre, g_ffn_post,
      wg_bf, wu_bf, wd_bf)


def kernel(x, c, w_cond, b_cond, g_mix_pre, g_mix_post, w_in, w_pool, pool_scale, w_out,
           g_ffn_pre, g_ffn_post, w_gate, w_up, w_down):
    B, S, D = x.shape
    depth = w_cond.shape[0]
    for l in range(depth):
        mod3 = _cond(c, w_cond[l], b_cond[l]).reshape(B, 1, N_MOD * D)
        q, k, vt, u = _proj(x, mod3, g_mix_pre[l].reshape(1, D), w_in[l].astype(BF16))
        attn = _attention(q, k, vt)
        x = _tail(attn, u, x, mod3, w_pool[l].astype(BF16), pool_scale[l].reshape(1, POOL_WIDTH),
                  w_out[l].astype(BF16), g_mix_post[l].reshape(1, D), g_ffn_pre[l].reshape(1, D),
                  g_ffn_post[l].reshape(1, D), w_gate[l].astype(BF16), w_up[l].astype(BF16),
                  w_down[l].astype(BF16))
    return x
```
